```python
import math
import jax, jax.numpy as jnp
from jax import lax
import numpy as np

D_MODEL = 2048
BATCH = 4
SEQ = 2048
DEPTH = 4

CHUNK = 64
Q_BLOCK = 128
ROPE_THETA = 500000.0
MLA_HEADS = 8
Q_LORA = 512
KV_LORA = 256
QK_NOPE = 128
QK_ROPE = 64
V_HEAD = 128
MLA_QK_DIM = QK_NOPE + QK_ROPE
DIFF_HEADS = 8
DIFF_D = 64
DIFF_ROT = DIFF_D // 4
DIFF_QK_COLS = DIFF_HEADS * 2 * DIFF_D
DIFF_V_COLS = DIFF_HEADS * 2 * DIFF_D
MLA_IN_COLS = Q_LORA + KV_LORA + QK_ROPE
IN_COLS = MLA_IN_COLS + 2 * DIFF_QK_COLS + DIFF_V_COLS
MIX_WIDTH = MLA_HEADS * V_HEAD + DIFF_HEADS * 2 * DIFF_D
D_FF = 5632
CONV_WIDTH = 3
PLE_DIM = 256
RMS_EPS = 1e-6

kernel_name = "hybrid_mla_diffattn_convglu_trunk"


def rms_norm(x, g):
    xf = x.astype(jnp.float32)
    y = xf * lax.rsqrt(jnp.mean(xf * xf, axis=-1, keepdims=True) + RMS_EPS)
    return (y * g.astype(jnp.float32)).astype(x.dtype)


def rope_tables(positions, rot_dim):
    inv = ROPE_THETA ** (-jnp.arange(0, rot_dim, 2, dtype=jnp.float32) / rot_dim)
    ang = positions.astype(jnp.float32)[..., None] * inv
    return jnp.cos(ang)[:, :, None, :], jnp.sin(ang)[:, :, None, :]


def apply_rope(x, cos, sin):
    half = cos.shape[-1]
    rot = 2 * half
    xf = x[..., :rot].astype(jnp.float32)
    x1, x2 = xf[..., :half], xf[..., half:]
    r = jnp.concatenate([x1 * cos - x2 * sin, x2 * cos + x1 * sin], axis=-1).astype(x.dtype)
    return jnp.concatenate([r, x[..., rot:]], axis=-1)


def chunk_causal_attention(q, k, v, scale):
    S = q.shape[1]
    outs = []
    for start in range(0, S, Q_BLOCK):
        end = start + Q_BLOCK
        s = jnp.einsum('bqhd,bkhd->bhqk', q[:, start:end], k[:, :end]).astype(jnp.float32) * scale
        q_chunk = (start + jnp.arange(Q_BLOCK)) // CHUNK
        k_chunk = jnp.arange(end) // CHUNK
        mask = k_chunk[None, :] <= q_chunk[:, None]
        s = jnp.where(mask[None, None], s, -1e30)
        pr = jax.nn.softmax(s, axis=-1).astype(v.dtype)
        outs.append(jnp.einsum('bhqk,bkhd->bqhd', pr, v[:, :end]))
    return jnp.concatenate(outs, axis=1)


def mla_mixer(c_q, c_kv, k_pe, g_q, w_uq, g_kv, w_ukv, cos, sin):
    B, S, _ = c_q.shape
    q = (rms_norm(c_q, g_q) @ w_uq).reshape(B, S, MLA_HEADS, MLA_QK_DIM)
    q = jnp.concatenate([q[..., :QK_NOPE], apply_rope(q[..., QK_NOPE:], cos, sin)], axis=-1)
    kv = (rms_norm(c_kv, g_kv) @ w_ukv).reshape(B, S, MLA_HEADS, QK_NOPE + V_HEAD)
    k_nope, v = kv[..., :QK_NOPE], kv[..., QK_NOPE:]
    k_rope = apply_rope(k_pe[:, :, None, :], cos, sin)
    k = jnp.concatenate([k_nope, jnp.broadcast_to(k_rope, (B, S, MLA_HEADS, QK_ROPE))], axis=-1)
    o = chunk_causal_attention(q, k, v, MLA_QK_DIM ** -0.5)
    return o.reshape(B, S, MLA_HEADS * V_HEAD)


def diff_mixer(dq, dk, dv, lq1, lk1, lq2, lk2, g_sub, lambda_init, cos, sin):
    B, S, _ = dq.shape
    q = apply_rope(dq.reshape(B, S, 2 * DIFF_HEADS, DIFF_D), cos, sin).reshape(B, S, DIFF_HEADS, 2, DIFF_D)
    k = apply_rope(dk.reshape(B, S, 2 * DIFF_HEADS, DIFF_D), cos, sin).reshape(B, S, DIFF_HEADS, 2, DIFF_D)
    v = dv.reshape(B, S, DIFF_HEADS, 2 * DIFF_D)
    lam = (jnp.exp(jnp.sum(lq1.astype(jnp.float32) * lk1.astype(jnp.float32)))
           - jnp.exp(jnp.sum(lq2.astype(jnp.float32) * lk2.astype(jnp.float32))) + lambda_init)
    scale = DIFF_D ** -0.5
    a1 = chunk_causal_attention(q[..., 0, :], k[..., 0, :], v, scale)
    a2 = chunk_causal_attention(q[..., 1, :], k[..., 1, :], v, scale)
    o = a1 - lam.astype(a1.dtype) * a2
    o = rms_norm(o, g_sub) * (1.0 - lambda_init)
    return o.reshape(B, S, DIFF_HEADS * 2 * DIFF_D)


def causal_dwconv(u, w, b):
    K, C = w.shape
    y = lax.conv_general_dilated(u, w[:, None, :].astype(u.dtype), window_strides=(1,),
                                 padding=[(K - 1, 0)], dimension_numbers=('NWC', 'WIO', 'NWC'),
                                 feature_group_count=C)
    return y + b


def conv_geglu_ffn(h, w_up, conv_w, conv_b, w_down):
    u = causal_dwconv(h @ w_up, conv_w, conv_b)
    gate, up = u[..., :D_FF], u[..., D_FF:]
    return (jax.nn.gelu(gate, approximate=True) * up) @ w_down


def setup_inputs(seed: int = 0) -> dict:
    key = jax.random.key(seed)
    ks = jax.random.split(key, 32)
    f32 = jnp.float32

    def w(k, shape, fan_in):
        return jax.random.normal(k, shape, f32) * (fan_in ** -0.5)

    def gain(k, shape):
        return 1.0 + 0.05 * jax.random.normal(k, shape, f32)

    offsets = jax.random.randint(ks[2], (BATCH,), 0, 64, dtype=jnp.int32) * CHUNK
    positions = offsets[:, None] + jnp.arange(SEQ, dtype=jnp.int32)[None, :]
    return {
        "x": jax.random.normal(ks[0], (BATCH, SEQ, D_MODEL), f32),
        "p": jax.random.normal(ks[1], (DEPTH, BATCH, SEQ, PLE_DIM), f32),
        "positions": positions,
        "g_mix_pre": gain(ks[3], (DEPTH, D_MODEL)),
        "w_in": w(ks[4], (DEPTH, D_MODEL, IN_COLS), D_MODEL),
        "g_q_lora": gain(ks[5], (DEPTH, Q_LORA)),
        "w_uq": w(ks[6], (DEPTH, Q_LORA, MLA_HEADS * MLA_QK_DIM), Q_LORA),
        "g_kv_lora": gain(ks[7], (DEPTH, KV_LORA)),
        "w_ukv": w(ks[8], (DEPTH, KV_LORA, MLA_HEADS * (QK_NOPE + V_HEAD)), KV_LORA),
        "lambda_q1": 0.1 * jax.random.normal(ks[9], (DEPTH, DIFF_D), f32),
        "lambda_k1": 0.1 * jax.random.normal(ks[10], (DEPTH, DIFF_D), f32),
        "lambda_q2": 0.1 * jax.random.normal(ks[11], (DEPTH, DIFF_D), f32),
        "lambda_k2": 0.1 * jax.random.normal(ks[12], (DEPTH, DIFF_D), f32),
        "g_diff_sub": gain(ks[13], (DEPTH, 2 * DIFF_D)),
        "w_o": w(ks[14], (DEPTH, MIX_WIDTH, D_MODEL), MIX_WIDTH),
        "g_mix_post": gain(ks[15], (DEPTH, D_MODEL)),
        "g_ffn_pre": gain(ks[16], (DEPTH, D_MODEL)),
        "w_up": w(ks[17], (DEPTH, D_MODEL, 2 * D_FF), D_MODEL),
        "conv_w": w(ks[18], (DEPTH, CONV_WIDTH, 2 * D_FF), CONV_WIDTH),
        "conv_b": 0.02 * jax.random.normal(ks[19], (DEPTH, 2 * D_FF), f32),
        "w_down": w(ks[20], (DEPTH, D_FF, D_MODEL), D_FF),
        "g_ffn_post": gain(ks[21], (DEPTH, D_MODEL)),
        "w_ple": w(ks[22], (DEPTH, PLE_DIM, D_MODEL), PLE_DIM),
        "w_ple_gate": w(ks[23], (DEPTH, D_MODEL, D_MODEL), D_MODEL),
        "g_ple": gain(ks[24], (DEPTH, D_MODEL)),
    }


def reference(x, p, positions, g_mix_pre, w_in, g_q_lora, w_uq, g_kv_lora, w_ukv,
              lambda_q1, lambda_k1, lambda_q2, lambda_k2, g_diff_sub, w_o, g_mix_post,
              g_ffn_pre, w_up, conv_w, conv_b, w_down, g_ffn_post, w_ple, w_ple_gate, g_ple):
    cos_mla, sin_mla = rope_tables(positions, QK_ROPE)
    cos_dif, sin_dif = rope_tables(positions, DIFF_ROT)
    cos_mla, sin_mla = cos_mla.astype(x.dtype), sin_mla.astype(x.dtype)
    cos_dif, sin_dif = cos_dif.astype(x.dtype), sin_dif.astype(x.dtype)
    split_at = [Q_LORA, Q_LORA + KV_LORA, MLA_IN_COLS, MLA_IN_COLS + DIFF_QK_COLS,
                MLA_IN_COLS + 2 * DIFF_QK_COLS]
    h = x
    for i in range(DEPTH):
        lambda_init = 0.8 - 0.6 * math.exp(-0.3 * i)
        z = rms_norm(h, g_mix_pre[i]) @ w_in[i]
        c_q, c_kv, k_pe, dq, dk, dv = jnp.split(z, split_at, axis=-1)
        mla_out = mla_mixer(c_q, c_kv, k_pe, g_q_lora[i], w_uq[i], g_kv_lora[i], w_ukv[i],
                            cos_mla, sin_mla)
        diff_out = diff_mixer(dq, dk, dv, lambda_q1[i], lambda_k1[i], lambda_q2[i], lambda_k2[i],
                              g_diff_sub[i], lambda_init, cos_dif, sin_dif)
        mix = jnp.concatenate([mla_out, diff_out], axis=-1) @ w_o[i]
        h = h + rms_norm(mix, g_mix_post[i])
        f = conv_geglu_ffn(rms_norm(h, g_ffn_pre[i]), w_up[i], conv_w[i], conv_b[i], w_down[i])
        h = h + rms_norm(f, g_ffn_post[i])
        ple = (p[i] @ w_ple[i]) * jax.nn.sigmoid(h @ w_ple_gate[i])
        h = h + rms_norm(ple, g_ple[i])
    return h
```

```python
import functools
import math

import jax
import jax.numpy as jnp
from jax import lax
from jax.experimental import pallas as pl
from jax.experimental.pallas import tpu as pltpu

D_MODEL = 2048
BATCH = 4
SEQ = 2048
DEPTH = 4
TOKENS = BATCH * SEQ
CHUNK = 64
ROPE_THETA = 500000.0
MLA_HEADS = 8
Q_LORA = 512
KV_LORA = 256
QK_NOPE = 128
QK_ROPE = 64
V_HEAD = 128
MLA_QK_DIM = QK_NOPE + QK_ROPE
DIFF_HEADS = 8
DIFF_D = 64
DIFF_ROT = DIFF_D // 4
DIFF_QK_COLS = DIFF_HEADS * 2 * DIFF_D
DIFF_V_COLS = DIFF_HEADS * 2 * DIFF_D
MLA_IN_COLS = Q_LORA + KV_LORA + QK_ROPE
DIFF_IN_COLS = 2 * DIFF_QK_COLS + DIFF_V_COLS
D_FF = 5632
CONV_WIDTH = 3
PLE_DIM = 256
RMS_EPS = 1e-6

LANES = 128
SUBLANES = 8
MLA_HEAD_PAD = 256
VMEM_LIMIT = 48 * 1024 * 1024

ROW_TILE = 512
ATTN_Q_TILE = 256
ATTN_K_TILE = 256
FF_TILE = 512
DIFF_N_TILE = 512

F32 = jnp.float32
BF16 = jnp.bfloat16
NEG = -1e30


def _rms(x, g):
    return x * lax.rsqrt(jnp.mean(x * x, axis=-1, keepdims=True) + RMS_EPS) * g


def _rope(x, c, sa, sb, half):
    return x * c + pltpu.roll(x, half, 1) * sa + pltpu.roll(x, LANES - half, 1) * sb


def _dot(a, b):
    return jnp.dot(a, b, preferred_element_type=F32)


def _params(sem):
    return pltpu.CompilerParams(dimension_semantics=sem, vmem_limit_bytes=VMEM_LIMIT)


def _mla_proj_kernel(h_ref, g_pre_ref, w_in_ref, g_q_ref, w_uq_ref, g_kv_ref, w_ukv_ref,
                     c_ref, sa_ref, sb_ref, xn_ref, q_ref, kv_ref, kr_ref):
    xn = _rms(h_ref[...], g_pre_ref[...]).astype(BF16)
    xn_ref[...] = xn
    z = _dot(xn, w_in_ref[...])
    c, sa, sb = c_ref[...], sa_ref[...], sb_ref[...]
    half = QK_ROPE // 2
    scale = MLA_QK_DIM ** -0.5

    cq = _rms(z[:, :Q_LORA], g_q_ref[...]).astype(BF16)
    q = _dot(cq, w_uq_ref[...])
    for hd in range(MLA_HEADS):
        base = hd * MLA_HEAD_PAD
        q_ref[:, base:base + QK_NOPE] = (q[:, base:base + QK_NOPE] * scale).astype(BF16)
        r = _rope(q[:, base + QK_NOPE:base + MLA_HEAD_PAD], c, sa, sb, half)
        q_ref[:, base + QK_NOPE:base + MLA_HEAD_PAD] = (r * scale).astype(BF16)

    ckv = _rms(z[:, Q_LORA:Q_LORA + KV_LORA], g_kv_ref[...]).astype(BF16)
    kv_ref[...] = _dot(ckv, w_ukv_ref[...]).astype(BF16)

    kpe = z[:, Q_LORA + KV_LORA:MLA_IN_COLS]
    kpe = jnp.concatenate([kpe, jnp.zeros_like(kpe)], axis=1)
    kr_ref[...] = _rope(kpe, c, sa, sb, half).astype(BF16)


def _mla_proj(l, h, g_pre, w_in_mla, g_q, w_uq, g_kv, w_ukv, tabs):
    tm = ROW_TILE
    row = lambda n: pl.BlockSpec((tm, n), lambda i: (i, 0))
    lay = lambda *s: pl.BlockSpec((None,) + s, lambda i: (l,) + (0,) * len(s))
    return pl.pallas_call(
        _mla_proj_kernel,
        grid=(TOKENS // tm,),
        in_specs=[row(D_MODEL), lay(1, D_MODEL), lay(D_MODEL, MLA_IN_COLS), lay(1, Q_LORA),
                  lay(Q_LORA, MLA_HEADS * MLA_HEAD_PAD), lay(1, KV_LORA),
                  lay(KV_LORA, MLA_HEADS * (QK_NOPE + V_HEAD)), row(LANES), row(LANES), row(LANES)],
        out_specs=[row(D_MODEL), row(MLA_HEADS * MLA_HEAD_PAD), row(MLA_HEADS * (QK_NOPE + V_HEAD)),
                   row(LANES)],
        out_shape=[jax.ShapeDtypeStruct((TOKENS, D_MODEL), BF16),
                   jax.ShapeDtypeStruct((TOKENS, MLA_HEADS * MLA_HEAD_PAD), BF16),
                   jax.ShapeDtypeStruct((TOKENS, MLA_HEADS * (QK_NOPE + V_HEAD)), BF16),
                   jax.ShapeDtypeStruct((TOKENS, LANES), BF16)],
        compiler_params=_params(("parallel",)),
        name="mla_proj",
    )(h, g_pre, w_in_mla, g_q, w_uq, g_kv, w_ukv, *tabs)


def _diff_proj_kernel(xn_ref, w_ref, c_ref, sa_ref, sb_ref, o_ref):
    j = pl.program_id(1)
    z = _dot(xn_ref[...], w_ref[...])
    q_blocks = DIFF_QK_COLS // DIFF_N_TILE
    half = DIFF_ROT // 2

    def roped(scale):
        c, sa, sb = c_ref[...], sa_ref[...], sb_ref[...]
        for t in range(DIFF_N_TILE // LANES):
            sl = slice(t * LANES, (t + 1) * LANES)
            o_ref[:, sl] = (_rope(z[:, sl], c, sa, sb, half) * scale).astype(BF16)

    @pl.when(j < q_blocks)
    def _():
        roped(DIFF_D ** -0.5)

    @pl.when(jnp.logical_and(j >= q_blocks, j < 2 * q_blocks))
    def _():
        roped(1.0)

    @pl.when(j >= 2 * q_blocks)
    def _():
        o_ref[...] = z.astype(BF16)


def _diff_proj(l, xn, w_in_diff, tabs):
    tm, tn = ROW_TILE, DIFF_N_TILE
    tab = pl.BlockSpec((tm, LANES), lambda i, j: (i, 0))
    return pl.pallas_call(
        _diff_proj_kernel,
        grid=(TOKENS // tm, DIFF_IN_COLS // tn),
        in_specs=[pl.BlockSpec((tm, D_MODEL), lambda i, j: (i, 0)),
                  pl.BlockSpec((None, D_MODEL, tn), lambda i, j: (l, 0, j)), tab, tab, tab],
        out_specs=pl.BlockSpec((tm, tn), lambda i, j: (i, j)),
        out_shape=jax.ShapeDtypeStruct((TOKENS, DIFF_IN_COLS), BF16),
        compiler_params=_params(("parallel", "parallel")),
        name="diff_proj",
    )(xn, w_in_diff, *tabs)


def _chunk_mask(rows, q_tile):
    r = lax.broadcasted_iota(jnp.int32, (rows, ATTN_K_TILE), 0) % q_tile
    c = lax.broadcasted_iota(jnp.int32, (rows, ATTN_K_TILE), 1)
    return (c // CHUNK) <= (r // CHUNK)


def _online_attention(q, load_k, load_v, qi, v_dim):
    rows = q.shape[0]

    def tile(j, carry, masked):
        m, l, acc = carry
        start = pl.multiple_of(j * ATTN_K_TILE, ATTN_K_TILE)
        s = lax.dot_general(q, load_k(start), (((1,), (1,)), ((), ())), preferred_element_type=F32)
        if masked:
            s = jnp.where(_chunk_mask(rows, ATTN_Q_TILE), s, NEG)
        m_new = jnp.maximum(m, jnp.max(s, axis=-1, keepdims=True))
        alpha = jnp.exp(m - m_new)
        p = jnp.exp(s - m_new)
        l = alpha * l + jnp.sum(p, axis=-1, keepdims=True)
        acc = alpha * acc + _dot(p.astype(BF16), load_v(start))
        return m_new, l, acc

    init = (jnp.full((rows, 1), NEG, F32), jnp.zeros((rows, 1), F32), jnp.zeros((rows, v_dim), F32))
    carry = lax.fori_loop(0, qi, lambda j, c: tile(j, c, False), init)
    _, l, acc = tile(qi, carry, True)
    return acc / l


def _mla_attn_kernel(q_ref, kv_ref, kr_ref, o_ref, kcat_ref):
    qi = pl.program_id(2)

    @pl.when(qi == 0)
    def _():
        kcat_ref[:, :QK_NOPE] = kv_ref[:, :QK_NOPE]
        kcat_ref[:, QK_NOPE:] = kr_ref[...]

    o = _online_attention(
        q_ref[...],
        lambda s: kcat_ref[pl.ds(s, ATTN_K_TILE), :],
        lambda s: kv_ref[pl.ds(s, ATTN_K_TILE), QK_NOPE:],
        qi, V_HEAD)
    o_ref[...] = o.astype(BF16)


def _mla_attn(q, kv, kr):
    tq = ATTN_Q_TILE
    nq = SEQ // tq
    return pl.pallas_call(
        _mla_attn_kernel,
        grid=(BATCH, MLA_HEADS, nq),
        in_specs=[pl.BlockSpec((tq, MLA_HEAD_PAD), lambda b, h, i: (b * nq + i, h)),
                  pl.BlockSpec((SEQ, QK_NOPE + V_HEAD), lambda b, h, i: (b, h)),
                  pl.BlockSpec((SEQ, LANES), lambda b, h, i: (b, 0))],
        out_specs=pl.BlockSpec((tq, V_HEAD), lambda b, h, i: (b * nq + i, h)),
        out_shape=jax.ShapeDtypeStruct((TOKENS, MLA_HEADS * V_HEAD), BF16),
        scratch_shapes=[pltpu.VMEM((SEQ, MLA_HEAD_PAD), BF16)],
        compiler_params=_params(("parallel", "parallel", "arbitrary")),
        name="mla_attn",
    )(q, kv, kr)


def _diff_attn_kernel(lambda_init, q_ref, k_ref, v_ref, lq1_ref, lk1_ref, lq2_ref, lk2_ref, g_ref, o_ref):
    qi = pl.program_id(2)
    tq = ATTN_Q_TILE
    q = q_ref[...]
    lane = lax.broadcasted_iota(jnp.int32, q.shape, 1)
    zero = jnp.zeros_like(q)
    qq = jnp.concatenate([jnp.where(lane < DIFF_D, q, zero), jnp.where(lane >= DIFF_D, q, zero)], axis=0)
    a = _online_attention(
        qq,
        lambda s: k_ref[pl.ds(s, ATTN_K_TILE), :],
        lambda s: v_ref[pl.ds(s, ATTN_K_TILE), :],
        qi, 2 * DIFF_D)
    lam = (jnp.exp(jnp.sum(lq1_ref[...] * lk1_ref[...], axis=-1, keepdims=True))
           - jnp.exp(jnp.sum(lq2_ref[...] * lk2_ref[...], axis=-1, keepdims=True)) + lambda_init)
    o = a[:tq] - lam * a[tq:]
    o_ref[...] = (_rms(o, g_ref[...]) * (1.0 - lambda_init)).astype(BF16)


def _diff_attn(l, lambda_init, zd, lq1, lk1, lq2, lk2, g_sub):
    tq = ATTN_Q_TILE
    nq = SEQ // tq
    lay = lambda n: pl.BlockSpec((None, 1, n), lambda b, h, i: (l, 0, 0))
    return pl.pallas_call(
        functools.partial(_diff_attn_kernel, lambda_init),
        grid=(BATCH, DIFF_HEADS, nq),
        in_specs=[pl.BlockSpec((tq, LANES), lambda b, h, i: (b * nq + i, h)),
                  pl.BlockSpec((SEQ, LANES), lambda b, h, i: (b, DIFF_HEADS + h)),
                  pl.BlockSpec((SEQ, LANES), lambda b, h, i: (b, 2 * DIFF_HEADS + h)),
                  lay(DIFF_D), lay(DIFF_D), lay(DIFF_D), lay(DIFF_D), lay(2 * DIFF_D)],
        out_specs=pl.BlockSpec((tq, 2 * DIFF_D), lambda b, h, i: (b * nq + i, h)),
        out_shape=jax.ShapeDtypeStruct((TOKENS, DIFF_V_COLS), BF16),
        compiler_params=_params(("parallel", "parallel", "arbitrary")),
        name="diff_attn",
    )(zd, zd, zd, lq1, lk1, lq2, lk2, g_sub)


def _out_proj_kernel(h_ref, a_ref, d_ref, wa_ref, wd_ref, g_ref, o_ref):
    mix = _dot(a_ref[...], wa_ref[...]) + _dot(d_ref[...], wd_ref[...])
    o_ref[...] = h_ref[...] + _rms(mix, g_ref[...])


def _out_proj(l, h, mla_out, diff_out, w_o, g_post):
    tm = ROW_TILE
    half = MLA_HEADS * V_HEAD
    row = lambda n: pl.BlockSpec((tm, n), lambda i: (i, 0))
    return pl.pallas_call(
        _out_proj_kernel,
        grid=(TOKENS // tm,),
        in_specs=[row(D_MODEL), row(half), row(DIFF_V_COLS),
                  pl.BlockSpec((None, half, D_MODEL), lambda i: (l, 0, 0)),
                  pl.BlockSpec((None, DIFF_V_COLS, D_MODEL), lambda i: (l, 1, 0)),
                  pl.BlockSpec((None, 1, D_MODEL), lambda i: (l, 0, 0))],
        out_specs=row(D_MODEL),
        out_shape=jax.ShapeDtypeStruct((TOKENS, D_MODEL), F32),
        compiler_params=_params(("parallel",)),
        name="out_proj",
    )(h, mla_out, diff_out, w_o, w_o, g_post)


def _ffn_kernel(h_ref, g_pre_ref, wg_ref, wu_ref, cwg_ref, cwu_ref, cbg_ref, cbu_ref, wd_ref, g_post_ref,
                o_ref, xn_ref, ug_ref, uu_ref, carry_g_ref, carry_u_ref):
    i, j = pl.program_id(0), pl.program_id(1)
    tm = ROW_TILE
    pad = SUBLANES
    seq_start = i % (SEQ // tm) == 0

    @pl.when(j == 0)
    def _():
        xn_ref[...] = _rms(h_ref[...], g_pre_ref[...]).astype(BF16)

    def conv_branch(w_ref, cw_ref, cb_ref, u_ref, carry_ref):
        u_ref[pad:, :] = _dot(xn_ref[...], w_ref[...])

        @pl.when(seq_start)
        def _():
            u_ref[:pad, :] = jnp.zeros((pad, FF_TILE), F32)

        @pl.when(jnp.logical_not(seq_start))
        def _():
            u_ref[:pad, :] = carry_ref[j]

        carry_ref[j] = u_ref[tm:, :]
        cw = cw_ref[...]
        return (cw[2:3] * u_ref[pad:, :] + cw[1:2] * u_ref[pad - 1:tm + pad - 1, :]
                + cw[0:1] * u_ref[pad - 2:tm + pad - 2, :] + cb_ref[...])

    gate = conv_branch(wg_ref, cwg_ref, cbg_ref, ug_ref, carry_g_ref)
    up = conv_branch(wu_ref, cwu_ref, cbu_ref, uu_ref, carry_u_ref)
    act = (jax.nn.gelu(gate, approximate=True) * up).astype(BF16)
    part = _dot(act, wd_ref[...])

    @pl.when(j == 0)
    def _():
        o_ref[...] = part

    @pl.when(j > 0)
    def _():
        o_ref[...] += part

    @pl.when(j == pl.num_programs(1) - 1)
    def _():
        o_ref[...] = h_ref[...] + _rms(o_ref[...], g_post_ref[...])


def _ffn(l, h, g_pre, w_up, conv_w, conv_b, w_down, g_post):
    tm, tf = ROW_TILE, FF_TILE
    nj = D_FF // tf
    row = pl.BlockSpec((tm, D_MODEL), lambda i, j: (i, 0))
    gain = pl.BlockSpec((None, 1, D_MODEL), lambda i, j: (l, 0, 0))
    col = lambda r, off: pl.BlockSpec((None, r, tf), lambda i, j: (l, 0, j + off))
    return pl.pallas_call(
        _ffn_kernel,
        grid=(TOKENS // tm, nj),
        in_specs=[row, gain, col(D_MODEL, 0), col(D_MODEL, nj), col(CONV_WIDTH, 0), col(CONV_WIDTH, nj),
                  col(1, 0), col(1, nj), pl.BlockSpec((None, tf, D_MODEL), lambda i, j: (l, j, 0)), gain],
        out_specs=row,
        out_shape=jax.ShapeDtypeStruct((TOKENS, D_MODEL), F32),
        scratch_shapes=[pltpu.VMEM((tm, D_MODEL), BF16),
                        pltpu.VMEM((tm + SUBLANES, tf), F32), pltpu.VMEM((tm + SUBLANES, tf), F32),
                        pltpu.VMEM((nj, SUBLANES, tf), F32), pltpu.VMEM((nj, SUBLANES, tf), F32)],
        compiler_params=_params(("arbitrary", "arbitrary")),
        name="ffn",
    )(h, g_pre, w_up, w_up, conv_w, conv_w, conv_b, conv_b, w_down, g_post)


def _ple_kernel(h_ref, p_ref, w_ple_ref, w_gate_ref, g_ref, o_ref):
    h = h_ref[...]
    gate = jax.nn.sigmoid(_dot(h.astype(BF16), w_gate_ref[...]))
    ple = _dot(p_ref[...].astype(BF16), w_ple_ref[...]) * gate
    o_ref[...] = h + _rms(ple, g_ref[...])


def _ple(l, h, p, w_ple, w_gate, g_ple):
    tm = ROW_TILE
    return pl.pallas_call(
        _ple_kernel,
        grid=(TOKENS // tm,),
        in_specs=[pl.BlockSpec((tm, D_MODEL), lambda i: (i, 0)),
                  pl.BlockSpec((None, tm, PLE_DIM), lambda i: (l, i, 0)),
                  pl.BlockSpec((None, PLE_DIM, D_MODEL), lambda i: (l, 0, 0)),
                  pl.BlockSpec((None, D_MODEL, D_MODEL), lambda i: (l, 0, 0)),
                  pl.BlockSpec((None, 1, D_MODEL), lambda i: (l, 0, 0))],
        out_specs=pl.BlockSpec((tm, D_MODEL), lambda i: (i, 0)),
        out_shape=jax.ShapeDtypeStruct((TOKENS, D_MODEL), F32),
        compiler_params=_params(("parallel",)),
        name="ple",
    )(h, p, w_ple, w_gate, g_ple)


def _rope_tables(positions, rot_dim, period):
    half = rot_dim // 2
    inv = ROPE_THETA ** (-jnp.arange(0, rot_dim, 2, dtype=F32) / rot_dim)
    ang = positions.astype(F32)[..., None] * inv
    cos = jnp.cos(ang).reshape(TOKENS, half)
    sin = jnp.sin(ang).reshape(TOKENS, half)
    rest = period - rot_dim
    ones, zeros, zh = jnp.ones((TOKENS, rest), F32), jnp.zeros((TOKENS, rest), F32), jnp.zeros_like(sin)
    c = jnp.concatenate([cos, cos, ones], axis=1)
    sa = jnp.concatenate([zh, sin, zeros], axis=1)
    sb = jnp.concatenate([-sin, zh, zeros], axis=1)
    reps = LANES // period
    return tuple(jnp.tile(t, (1, reps)) for t in (c, sa, sb))


def kernel(x, p, positions, g_mix_pre, w_in, g_q_lora, w_uq, g_kv_lora, w_ukv, lambda_q1, lambda_k1,
           lambda_q2, lambda_k2, g_diff_sub, w_o, g_mix_post, g_ffn_pre, w_up, conv_w, conv_b, w_down,
           g_ffn_post, w_ple, w_ple_gate, g_ple):
    tabs_mla = _rope_tables(positions, QK_ROPE, LANES)
    tabs_dif = _rope_tables(positions, DIFF_ROT, DIFF_D)

    vec = lambda a: a.reshape(DEPTH, 1, a.shape[-1])
    w_in_mla = w_in[:, :, :MLA_IN_COLS].astype(BF16)
    w_in_diff = w_in[:, :, MLA_IN_COLS:].astype(BF16)
    uq = w_uq.reshape(DEPTH, Q_LORA, MLA_HEADS, MLA_QK_DIM)
    uq = jnp.pad(uq, ((0, 0), (0, 0), (0, 0), (0, MLA_HEAD_PAD - MLA_QK_DIM)))
    w_uq_p = uq.reshape(DEPTH, Q_LORA, MLA_HEADS * MLA_HEAD_PAD).astype(BF16)
    w_ukv_b = w_ukv.astype(BF16)
    w_o_b = w_o.astype(BF16)
    w_up_b = w_up.astype(BF16)
    w_down_b = w_down.astype(BF16)
    w_ple_b = w_ple.astype(BF16)
    w_gate_b = w_ple_gate.astype(BF16)
    p2 = p.reshape(DEPTH, TOKENS, PLE_DIM)
    g_mix_pre, g_q_lora, g_kv_lora, g_diff_sub, g_mix_post, g_ffn_pre, g_ffn_post, g_ple, conv_b = map(
        vec, (g_mix_pre, g_q_lora, g_kv_lora, g_diff_sub, g_mix_post, g_ffn_pre, g_ffn_post, g_ple, conv_b))
    lambda_q1, lambda_k1, lambda_q2, lambda_k2 = map(vec, (lambda_q1, lambda_k1, lambda_q2, lambda_k2))

    h = x.reshape(TOKENS, D_MODEL)
    for l in range(DEPTH):
        lambda_init = 0.8 - 0.6 * math.exp(-0.3 * l)
        xn, q, kv, kr = _mla_proj(l, h, g_mix_pre, w_in_mla, g_q_lora, w_uq_p, g_kv_lora, w_ukv_b, tabs_mla)
        zd = _diff_proj(l, xn, w_in_diff, tabs_dif)
        mla_out = _mla_attn(q, kv, kr)
        diff_out = _diff_attn(l, lambda_init, zd, lambda_q1, lambda_k1, lambda_q2, lambda_k2, g_diff_sub)
        h = _out_proj(l, h, mla_out, diff_out, w_o_b, g_mix_post)
        h = _ffn(l, h, g_ffn_pre, w_up_b, conv_w, conv_b, w_down_b, g_ffn_post)
        h = _ple(l, h, p2, w_ple_b, w_gate_b, g_ple)
    return h.reshape(BATCH, SEQ, D_MODEL)
```

```python
import functools
import math

import jax
import jax.numpy as jnp
from jax import lax
from jax.experimental import pallas as pl
from jax.experimental.pallas import tpu as pltpu

D_MODEL = 2048
BATCH = 4
SEQ = 2048
DEPTH = 4
TOKENS = BATCH * SEQ
CHUNK = 64
ROPE_THETA = 500000.0
MLA_HEADS = 8
Q_LORA = 512
KV_LORA = 256
QK_NOPE = 128
QK_ROPE = 64
V_HEAD = 128
MLA_QK_DIM = QK_NOPE + QK_ROPE
DIFF_HEADS = 8
DIFF_D = 64
DIFF_ROT = DIFF_D // 4
DIFF_QK_COLS = DIFF_HEADS * 2 * DIFF_D
DIFF_V_COLS = DIFF_HEADS * 2 * DIFF_D
MLA_IN_COLS = Q_LORA + KV_LORA + QK_ROPE
MLA_IN_PAD = 1024
DIFF_IN_COLS = 2 * DIFF_QK_COLS + DIFF_V_COLS
D_FF = 5632
CONV_WIDTH = 3
PLE_DIM = 256
RMS_EPS = 1e-6

LANES = 128
SUBLANES = 8
MXU_COLS = 256
MLA_HEAD_PAD = 256
VMEM_LIMIT = 48 * 1024 * 1024

ROW_TILE = 512
ATTN_Q_TILE = 256
FF_TILE = 512
FF_SUBTILE = 256
DIFF_N_TILE = 512

F32 = jnp.float32
BF16 = jnp.bfloat16
NEG = -1e30


def _rms(x, g):
    return x * lax.rsqrt(jnp.mean(x * x, axis=-1, keepdims=True) + RMS_EPS) * g


def _rope(x, c, sa, sb, half):
    return x * c + pltpu.roll(x, half, 1) * sa + pltpu.roll(x, LANES - half, 1) * sb


def _dot(a, b):
    return jnp.dot(a, b, preferred_element_type=F32)


def _params(sem):
    return pltpu.CompilerParams(dimension_semantics=sem, vmem_limit_bytes=VMEM_LIMIT)


def _mla_proj_kernel(h_ref, g_pre_ref, w_in_ref, g_q_ref, w_uq_ref, g_kv_ref, w_ukv_ref,
                     c_ref, sa_ref, sb_ref, xn_ref, q_ref, kv_ref, kr_ref):
    xn = _rms(h_ref[...], g_pre_ref[...]).astype(BF16)
    xn_ref[...] = xn
    z = _dot(xn, w_in_ref[...])
    c, sa, sb = c_ref[...], sa_ref[...], sb_ref[...]
    half = QK_ROPE // 2
    scale = MLA_QK_DIM ** -0.5

    cq = _rms(z[:, :Q_LORA], g_q_ref[...]).astype(BF16)
    q = _dot(cq, w_uq_ref[...])
    for hd in range(MLA_HEADS):
        base = hd * MLA_HEAD_PAD
        q_ref[:, base:base + QK_NOPE] = (q[:, base:base + QK_NOPE] * scale).astype(BF16)
        r = _rope(q[:, base + QK_NOPE:base + MLA_HEAD_PAD], c, sa, sb, half)
        q_ref[:, base + QK_NOPE:base + MLA_HEAD_PAD] = (r * scale).astype(BF16)

    ckv = _rms(z[:, Q_LORA:Q_LORA + KV_LORA], g_kv_ref[...]).astype(BF16)
    kv_ref[...] = _dot(ckv, w_ukv_ref[...]).astype(BF16)

    kpe = z[:, Q_LORA + KV_LORA:MLA_IN_COLS]
    kpe = jnp.concatenate([kpe, jnp.zeros_like(kpe)], axis=1)
    kr_ref[...] = _rope(kpe, c, sa, sb, half).astype(BF16)


def _mla_proj(l, h, g_pre, w_in_mla, g_q, w_uq, g_kv, w_ukv, tabs):
    tm = ROW_TILE
    row = lambda n: pl.BlockSpec((tm, n), lambda i: (i, 0))
    lay = lambda *s: pl.BlockSpec((None,) + s, lambda i: (l,) + (0,) * len(s))
    return pl.pallas_call(
        _mla_proj_kernel,
        grid=(TOKENS // tm,),
        in_specs=[row(D_MODEL), lay(1, D_MODEL), lay(D_MODEL, MLA_IN_PAD), lay(1, Q_LORA),
                  lay(Q_LORA, MLA_HEADS * MLA_HEAD_PAD), lay(1, KV_LORA),
                  lay(KV_LORA, MLA_HEADS * (QK_NOPE + V_HEAD)), row(LANES), row(LANES), row(LANES)],
        out_specs=[row(D_MODEL), row(MLA_HEADS * MLA_HEAD_PAD), row(MLA_HEADS * (QK_NOPE + V_HEAD)),
                   row(LANES)],
        out_shape=[jax.ShapeDtypeStruct((TOKENS, D_MODEL), BF16),
                   jax.ShapeDtypeStruct((TOKENS, MLA_HEADS * MLA_HEAD_PAD), BF16),
                   jax.ShapeDtypeStruct((TOKENS, MLA_HEADS * (QK_NOPE + V_HEAD)), BF16),
                   jax.ShapeDtypeStruct((TOKENS, LANES), BF16)],
        compiler_params=_params(("parallel",)),
        name="mla_proj",
    )(h, g_pre, w_in_mla, g_q, w_uq, g_kv, w_ukv, *tabs)


def _diff_proj_kernel(xn_ref, w_ref, c_ref, sa_ref, sb_ref, o_ref):
    j = pl.program_id(1)
    q_blocks = DIFF_QK_COLS // DIFF_N_TILE
    half = DIFF_ROT // 2
    is_v = j >= 2 * q_blocks
    scale = jnp.where(j < q_blocks, DIFF_D ** -0.5, 1.0)
    c = jnp.where(is_v, 1.0, c_ref[...]) * scale
    sa = jnp.where(is_v, 0.0, sa_ref[...]) * scale
    sb = jnp.where(is_v, 0.0, sb_ref[...]) * scale
    for t in range(DIFF_N_TILE // MXU_COLS):
        z = _dot(xn_ref[...], w_ref[:, t * MXU_COLS:(t + 1) * MXU_COLS])
        for u in range(MXU_COLS // LANES):
            lo = t * MXU_COLS + u * LANES
            o_ref[:, lo:lo + LANES] = _rope(z[:, u * LANES:(u + 1) * LANES], c, sa, sb, half).astype(BF16)


def _diff_proj(l, xn, w_in_diff, tabs):
    tm, tn = ROW_TILE, DIFF_N_TILE
    tab = pl.BlockSpec((tm, LANES), lambda i, j: (i, 0))
    return pl.pallas_call(
        _diff_proj_kernel,
        grid=(TOKENS // tm, DIFF_IN_COLS // tn),
        in_specs=[pl.BlockSpec((tm, D_MODEL), lambda i, j: (i, 0)),
                  pl.BlockSpec((None, D_MODEL, tn), lambda i, j: (l, 0, MLA_IN_PAD // tn + j)), tab, tab, tab],
        out_specs=pl.BlockSpec((tm, tn), lambda i, j: (i, j)),
        out_shape=jax.ShapeDtypeStruct((TOKENS, DIFF_IN_COLS), BF16),
        compiler_params=_params(("parallel", "parallel")),
        name="diff_proj",
    )(xn, w_in_diff, *tabs)


def _chunk_mask(rows):
    r = lax.broadcasted_iota(jnp.int32, (rows, ATTN_Q_TILE), 0) % ATTN_Q_TILE
    c = lax.broadcasted_iota(jnp.int32, (rows, ATTN_Q_TILE), 1)
    return (c // CHUNK) <= (r // CHUNK)


def _prefix_attention(q, load_k, load_v, qi):
    nt = (((1,), (1,)), ((), ()))
    d0 = qi * ATTN_Q_TILE
    d1 = d0 + ATTN_Q_TILE
    s_d = lax.dot_general(q, load_k(d0, d1), nt, preferred_element_type=F32)
    s_d = jnp.where(_chunk_mask(q.shape[0]), s_d, NEG)
    m = jnp.max(s_d, axis=-1, keepdims=True)
    if qi > 0:
        s_p = lax.dot_general(q, load_k(0, d0), nt, preferred_element_type=F32)
        m = jnp.maximum(m, jnp.max(s_p, axis=-1, keepdims=True))
    p_d = jnp.exp(s_d - m)
    l = jnp.sum(p_d, axis=-1, keepdims=True)
    acc = _dot(p_d.astype(BF16), load_v(d0, d1))
    if qi > 0:
        p_p = jnp.exp(s_p - m)
        l = l + jnp.sum(p_p, axis=-1, keepdims=True)
        acc = acc + _dot(p_p.astype(BF16), load_v(0, d0))
    return acc / l


def _mla_attn_kernel(q_ref, kv_ref, kr_ref, o_ref, kcat_ref):
    kcat_ref[:, :QK_NOPE] = kv_ref[:, :QK_NOPE]
    kcat_ref[:, QK_NOPE:] = kr_ref[...]
    for qi in range(SEQ // ATTN_Q_TILE):
        rows = slice(qi * ATTN_Q_TILE, (qi + 1) * ATTN_Q_TILE)
        o = _prefix_attention(q_ref[rows, :], lambda a, b: kcat_ref[a:b, :],
                              lambda a, b: kv_ref[a:b, QK_NOPE:], qi)
        o_ref[rows, :] = o.astype(BF16)


def _mla_attn(q, kv, kr):
    return pl.pallas_call(
        _mla_attn_kernel,
        grid=(BATCH, MLA_HEADS),
        in_specs=[pl.BlockSpec((SEQ, MLA_HEAD_PAD), lambda b, h: (b, h)),
                  pl.BlockSpec((SEQ, QK_NOPE + V_HEAD), lambda b, h: (b, h)),
                  pl.BlockSpec((SEQ, LANES), lambda b, h: (b, 0))],
        out_specs=pl.BlockSpec((SEQ, V_HEAD), lambda b, h: (b, h)),
        out_shape=jax.ShapeDtypeStruct((TOKENS, MLA_HEADS * V_HEAD), BF16),
        scratch_shapes=[pltpu.VMEM((SEQ, MLA_HEAD_PAD), BF16)],
        compiler_params=_params(("parallel", "parallel")),
        name="mla_attn",
    )(q, kv, kr)


def _diff_attn_kernel(lambda_init, q_ref, k_ref, v_ref, lq1_ref, lk1_ref, lq2_ref, lk2_ref, g_ref, o_ref):
    tq = ATTN_Q_TILE
    lam = (jnp.exp(jnp.sum(lq1_ref[...] * lk1_ref[...], axis=-1, keepdims=True))
           - jnp.exp(jnp.sum(lq2_ref[...] * lk2_ref[...], axis=-1, keepdims=True)) + lambda_init)
    lane = lax.broadcasted_iota(jnp.int32, (tq, LANES), 1)
    zero = jnp.zeros((tq, LANES), BF16)
    for qi in range(SEQ // tq):
        rows = slice(qi * tq, (qi + 1) * tq)
        q = q_ref[rows, :]
        qq = jnp.concatenate([jnp.where(lane < DIFF_D, q, zero), jnp.where(lane >= DIFF_D, q, zero)], axis=0)
        a = _prefix_attention(qq, lambda a, b: k_ref[a:b, :], lambda a, b: v_ref[a:b, :], qi)
        o = a[:tq] - lam * a[tq:]
        o_ref[rows, :] = (_rms(o, g_ref[...]) * (1.0 - lambda_init)).astype(BF16)


def _diff_attn(l, lambda_init, zd, lq1, lk1, lq2, lk2, g_sub):
    lay = lambda n: pl.BlockSpec((None, 1, n), lambda b, h: (l, 0, 0))
    return pl.pallas_call(
        functools.partial(_diff_attn_kernel, lambda_init),
        grid=(BATCH, DIFF_HEADS),
        in_specs=[pl.BlockSpec((SEQ, LANES), lambda b, h: (b, h)),
                  pl.BlockSpec((SEQ, LANES), lambda b, h: (b, DIFF_HEADS + h)),
                  pl.BlockSpec((SEQ, LANES), lambda b, h: (b, 2 * DIFF_HEADS + h)),
                  lay(DIFF_D), lay(DIFF_D), lay(DIFF_D), lay(DIFF_D), lay(2 * DIFF_D)],
        out_specs=pl.BlockSpec((SEQ, 2 * DIFF_D), lambda b, h: (b, h)),
        out_shape=jax.ShapeDtypeStruct((TOKENS, DIFF_V_COLS), BF16),
        compiler_params=_params(("parallel", "parallel")),
        name="diff_attn",
    )(zd, zd, zd, lq1, lk1, lq2, lk2, g_sub)


def _out_proj_kernel(h_ref, a_ref, d_ref, wa_ref, wd_ref, g_ref, o_ref):
    mix = _dot(a_ref[...], wa_ref[...]) + _dot(d_ref[...], wd_ref[...])
    o_ref[...] = h_ref[...] + _rms(mix, g_ref[...])


def _out_proj(l, h, mla_out, diff_out, w_o, g_post):
    tm = ROW_TILE
    half = MLA_HEADS * V_HEAD
    row = lambda n: pl.BlockSpec((tm, n), lambda i: (i, 0))
    return pl.pallas_call(
        _out_proj_kernel,
        grid=(TOKENS // tm,),
        in_specs=[row(D_MODEL), row(half), row(DIFF_V_COLS),
                  pl.BlockSpec((None, half, D_MODEL), lambda i: (l, 0, 0)),
                  pl.BlockSpec((None, DIFF_V_COLS, D_MODEL), lambda i: (l, 1, 0)),
                  pl.BlockSpec((None, 1, D_MODEL), lambda i: (l, 0, 0))],
        out_specs=row(D_MODEL),
        out_shape=jax.ShapeDtypeStruct((TOKENS, D_MODEL), F32),
        compiler_params=_params(("parallel",)),
        name="out_proj",
    )(h, mla_out, diff_out, w_o, w_o, g_post)


def _ffn_kernel(h_ref, g_pre_ref, wg_ref, wu_ref, cwg_ref, cwu_ref, cbg_ref, cbu_ref, wd_ref, g_post_ref,
                o_ref, xn_ref, ug_ref, uu_ref, act_ref, carry_g_ref, carry_u_ref):
    i, j = pl.program_id(0), pl.program_id(1)
    tm = ROW_TILE
    pad = SUBLANES
    seq_start = i % (SEQ // tm) == 0

    @pl.when(j == 0)
    def _():
        xn_ref[...] = _rms(h_ref[...], g_pre_ref[...]).astype(BF16)
        o_ref[...] = jnp.zeros((tm, D_MODEL), F32)

    @pl.when(jnp.logical_and(i == 0, j == 0))
    def _():
        carry_g_ref[...] = jnp.zeros(carry_g_ref.shape, F32)
        carry_u_ref[...] = jnp.zeros(carry_u_ref.shape, F32)

    def conv_branch(cols, w_ref, cw_ref, cb_ref, u_ref, carry_ref):
        u_ref[pad:, cols] = _dot(xn_ref[...], w_ref[:, cols])
        u_ref[:pad, cols] = jnp.where(seq_start, 0.0, carry_ref[j, :, cols])
        carry_ref[j, :, cols] = u_ref[tm:, cols]
        return (cw_ref[2:3, cols] * u_ref[pad:, cols] + cw_ref[1:2, cols] * u_ref[pad - 1:tm + pad - 1, cols]
                + cw_ref[0:1, cols] * u_ref[pad - 2:tm + pad - 2, cols] + cb_ref[:, cols])

    for c in range(FF_TILE // FF_SUBTILE):
        cols = slice(c * FF_SUBTILE, (c + 1) * FF_SUBTILE)
        gate = conv_branch(cols, wg_ref, cwg_ref, cbg_ref, ug_ref, carry_g_ref)
        up = conv_branch(cols, wu_ref, cwu_ref, cbu_ref, uu_ref, carry_u_ref)
        act_ref[:, cols] = (jax.nn.gelu(gate, approximate=True) * up).astype(BF16)
    o_ref[...] += _dot(act_ref[...], wd_ref[...])

    @pl.when(j == pl.num_programs(1) - 1)
    def _():
        o_ref[...] = h_ref[...] + _rms(o_ref[...], g_post_ref[...])


def _ffn(l, h, g_pre, w_up, conv_w, conv_b, w_down, g_post):
    tm, tf = ROW_TILE, FF_TILE
    nj = D_FF // tf
    row = pl.BlockSpec((tm, D_MODEL), lambda i, j: (i, 0))
    gain = pl.BlockSpec((None, 1, D_MODEL), lambda i, j: (l, 0, 0))
    col = lambda r, off: pl.BlockSpec((None, r, tf), lambda i, j: (l, 0, j + off))
    return pl.pallas_call(
        _ffn_kernel,
        grid=(TOKENS // tm, nj),
        in_specs=[row, gain, col(D_MODEL, 0), col(D_MODEL, nj), col(CONV_WIDTH, 0), col(CONV_WIDTH, nj),
                  col(1, 0), col(1, nj), pl.BlockSpec((None, tf, D_MODEL), lambda i, j: (l, j, 0)), gain],
        out_specs=row,
        out_shape=jax.ShapeDtypeStruct((TOKENS, D_MODEL), F32),
        scratch_shapes=[pltpu.VMEM((tm, D_MODEL), BF16),
                        pltpu.VMEM((tm + SUBLANES, tf), F32), pltpu.VMEM((tm + SUBLANES, tf), F32),
                        pltpu.VMEM((tm, tf), BF16),
                        pltpu.VMEM((nj, SUBLANES, tf), F32), pltpu.VMEM((nj, SUBLANES, tf), F32)],
        compiler_params=_params(("arbitrary", "arbitrary")),
        name="ffn",
    )(h, g_pre, w_up, w_up, conv_w, conv_w, conv_b, conv_b, w_down, g_post)


def _ple_kernel(h_ref, p_ref, w_ple_ref, w_gate_ref, g_ref, o_ref):
    h = h_ref[...]
    gate = jax.nn.sigmoid(_dot(h.astype(BF16), w_gate_ref[...]))
    ple = _dot(p_ref[...].astype(BF16), w_ple_ref[...]) * gate
    o_ref[...] = h + _rms(ple, g_ref[...])


def _ple(l, h, p, w_ple, w_gate, g_ple):
    tm = ROW_TILE
    return pl.pallas_call(
        _ple_kernel,
        grid=(TOKENS // tm,),
        in_specs=[pl.BlockSpec((tm, D_MODEL), lambda i: (i, 0)),
                  pl.BlockSpec((None, tm, PLE_DIM), lambda i: (l, i, 0)),
                  pl.BlockSpec((None, PLE_DIM, D_MODEL), lambda i: (l, 0, 0)),
                  pl.BlockSpec((None, D_MODEL, D_MODEL), lambda i: (l, 0, 0)),
                  pl.BlockSpec((None, 1, D_MODEL), lambda i: (l, 0, 0))],
        out_specs=pl.BlockSpec((tm, D_MODEL), lambda i: (i, 0)),
        out_shape=jax.ShapeDtypeStruct((TOKENS, D_MODEL), F32),
        compiler_params=_params(("parallel",)),
        name="ple",
    )(h, p, w_ple, w_gate, g_ple)


def _rope_tables(positions, rot_dim, period):
    half = rot_dim // 2
    inv = ROPE_THETA ** (-jnp.arange(0, rot_dim, 2, dtype=F32) / rot_dim)
    ang = positions.astype(F32)[..., None] * inv
    cos = jnp.cos(ang).reshape(TOKENS, half)
    sin = jnp.sin(ang).reshape(TOKENS, half)
    rest = period - rot_dim
    ones, zeros, zh = jnp.ones((TOKENS, rest), F32), jnp.zeros((TOKENS, rest), F32), jnp.zeros_like(sin)
    c = jnp.concatenate([cos, cos, ones], axis=1)
    sa = jnp.concatenate([zh, sin, zeros], axis=1)
    sb = jnp.concatenate([-sin, zh, zeros], axis=1)
    reps = LANES // period
    return tuple(jnp.tile(t, (1, reps)) for t in (c, sa, sb))


def kernel(x, p, positions, g_mix_pre, w_in, g_q_lora, w_uq, g_kv_lora, w_ukv, lambda_q1, lambda_k1,
           lambda_q2, lambda_k2, g_diff_sub, w_o, g_mix_post, g_ffn_pre, w_up, conv_w, conv_b, w_down,
           g_ffn_post, w_ple, w_ple_gate, g_ple):
    tabs_mla = _rope_tables(positions, QK_ROPE, LANES)
    tabs_dif = _rope_tables(positions, DIFF_ROT, DIFF_D)

    vec = lambda a: a.reshape(DEPTH, 1, a.shape[-1])
    w_in_p = jnp.concatenate(
        [w_in[:, :, :MLA_IN_COLS], jnp.zeros((DEPTH, D_MODEL, MLA_IN_PAD - MLA_IN_COLS), F32),
         w_in[:, :, MLA_IN_COLS:]], axis=-1).astype(BF16)
    uq = w_uq.reshape(DEPTH, Q_LORA, MLA_HEADS, MLA_QK_DIM)
    uq = jnp.pad(uq, ((0, 0), (0, 0), (0, 0), (0, MLA_HEAD_PAD - MLA_QK_DIM)))
    w_uq_p = uq.reshape(DEPTH, Q_LORA, MLA_HEADS * MLA_HEAD_PAD).astype(BF16)
    w_ukv_b = w_ukv.astype(BF16)
    w_o_b = w_o.astype(BF16)
    w_up_b = w_up.astype(BF16)
    w_down_b = w_down.astype(BF16)
    w_ple_b = w_ple.astype(BF16)
    w_gate_b = w_ple_gate.astype(BF16)
    p2 = p.reshape(DEPTH, TOKENS, PLE_DIM)
    g_mix_pre, g_q_lora, g_kv_lora, g_diff_sub, g_mix_post, g_ffn_pre, g_ffn_post, g_ple, conv_b = map(
        vec, (g_mix_pre, g_q_lora, g_kv_lora, g_diff_sub, g_mix_post, g_ffn_pre, g_ffn_post, g_ple, conv_b))
    lambda_q1, lambda_k1, lambda_q2, lambda_k2 = map(vec, (lambda_q1, lambda_k1, lambda_q2, lambda_k2))

    h = x.reshape(TOKENS, D_MODEL)
    for l in range(DEPTH):
        lambda_init = 0.8 - 0.6 * math.exp(-0.3 * l)
        xn, q, kv, kr = _mla_proj(l, h, g_mix_pre, w_in_p, g_q_lora, w_uq_p, g_kv_lora, w_ukv_b, tabs_mla)
        zd = _diff_proj(l, xn, w_in_p, tabs_dif)
        mla_out = _mla_attn(q, kv, kr)
        diff_out = _diff_attn(l, lambda_init, zd, lambda_q1, lambda_k1, lambda_q2, lambda_k2, g_diff_sub)
        h = _out_proj(l, h, mla_out, diff_out, w_o_b, g_mix_post)
        h = _ffn(l, h, g_ffn_pre, w_up_b, conv_w, conv_b, w_down_b, g_ffn_post)
        h = _ple(l, h, p2, w_ple_b, w_gate_b, g_ple)
    return h.reshape(BATCH, SEQ, D_MODEL)
```

```python
import functools
import math

import jax
import jax.numpy as jnp
from jax import lax
from jax.experimental import pallas as pl
from jax.experimental.pallas import tpu as pltpu

D_MODEL = 2048
BATCH = 4
SEQ = 2048
DEPTH = 4
TOKENS = BATCH * SEQ
CHUNK = 64
ROPE_THETA = 500000.0
MLA_HEADS = 8
Q_LORA = 512
KV_LORA = 256
QK_NOPE = 128
QK_ROPE = 64
V_HEAD = 128
MLA_QK_DIM = QK_NOPE + QK_ROPE
DIFF_HEADS = 8
DIFF_D = 64
DIFF_ROT = DIFF_D // 4
DIFF_QK_COLS = DIFF_HEADS * 2 * DIFF_D
DIFF_V_COLS = DIFF_HEADS * 2 * DIFF_D
MLA_IN_COLS = Q_LORA + KV_LORA + QK_ROPE
MLA_IN_PAD = 1024
DIFF_IN_COLS = 2 * DIFF_QK_COLS + DIFF_V_COLS
D_FF = 5632
CONV_WIDTH = 3
PLE_DIM = 256
RMS_EPS = 1e-6

LANES = 128
SUBLANES = 8
MXU_COLS = 256
MLA_HEAD_PAD = 256
VMEM_LIMIT = 48 * 1024 * 1024

ROW_TILE = 512
ATTN_Q_TILE = 256
ATTN_HEADS = 4
DIFF_ATTN_HEADS = 2
FF_TILE = 512
FF_SUBTILE = 256

F32 = jnp.float32
BF16 = jnp.bfloat16
NEG = -1e30
LOG2E = math.log2(math.e)


def _rms(x, g):
    return x * lax.rsqrt(jnp.mean(x * x, axis=-1, keepdims=True) + RMS_EPS) * g


def _rope(x, c, sa, sb, half):
    return x * c + pltpu.roll(x, half, 1) * sa + pltpu.roll(x, LANES - half, 1) * sb


def _dot(a, b):
    return jnp.dot(a, b, preferred_element_type=F32)


def _params(sem):
    return pltpu.CompilerParams(dimension_semantics=sem, vmem_limit_bytes=VMEM_LIMIT)


def _in_proj_kernel(h_ref, g_pre_ref, w_in_ref, g_q_ref, w_uq_ref, g_kv_ref, w_ukv_ref,
                    cm_ref, sam_ref, sbm_ref, cd_ref, sad_ref, sbd_ref,
                    q_ref, kv_ref, kr_ref, zd_ref, xn_ref):
    xn_ref[...] = _rms(h_ref[...], g_pre_ref[...]).astype(BF16)

    z = _dot(xn_ref[...], w_in_ref[:, :MLA_IN_PAD])
    tabs = cm_ref[...], sam_ref[...], sbm_ref[...]
    half = QK_ROPE // 2
    scale = MLA_QK_DIM ** -0.5 * LOG2E
    cq = _rms(z[:, :Q_LORA], g_q_ref[...]).astype(BF16)
    for hd in range(MLA_HEADS):
        base = hd * MLA_HEAD_PAD
        qh = _dot(cq, w_uq_ref[:, base:base + MLA_HEAD_PAD])
        q_ref[:, base:base + QK_NOPE] = (qh[:, :QK_NOPE] * scale).astype(BF16)
        q_ref[:, base + QK_NOPE:base + MLA_HEAD_PAD] = (_rope(qh[:, QK_NOPE:], *tabs, half) * scale).astype(BF16)
    ckv = _rms(z[:, Q_LORA:Q_LORA + KV_LORA], g_kv_ref[...]).astype(BF16)
    kv_ref[...] = _dot(ckv, w_ukv_ref[...]).astype(BF16)
    kpe = z[:, Q_LORA + KV_LORA:MLA_IN_COLS]
    kpe = jnp.concatenate([kpe, jnp.zeros_like(kpe)], axis=1)
    kr_ref[...] = _rope(kpe, *tabs, half).astype(BF16)

    k_tabs = cd_ref[...], sad_ref[...], sbd_ref[...]
    q_tabs = tuple(t * (DIFF_D ** -0.5 * LOG2E) for t in k_tabs)
    half = DIFF_ROT // 2
    for t in range(DIFF_IN_COLS // MXU_COLS):
        lo = t * MXU_COLS
        zt = _dot(xn_ref[...], w_in_ref[:, MLA_IN_PAD + lo:MLA_IN_PAD + lo + MXU_COLS])
        if lo >= 2 * DIFF_QK_COLS:
            zd_ref[:, lo:lo + MXU_COLS] = zt.astype(BF16)
        else:
            tabs = q_tabs if lo < DIFF_QK_COLS else k_tabs
            for u in range(MXU_COLS // LANES):
                zd_ref[:, lo + u * LANES:lo + (u + 1) * LANES] = _rope(
                    zt[:, u * LANES:(u + 1) * LANES], *tabs, half).astype(BF16)


def _in_proj(l, h, g_pre, w_in_p, g_q, w_uq, g_kv, w_ukv, tabs_mla, tabs_dif):
    tm = ROW_TILE
    row = lambda n: pl.BlockSpec((tm, n), lambda i: (i, 0))
    lay = lambda *s: pl.BlockSpec((None,) + s, lambda i: (l,) + (0,) * len(s), pipeline_mode=pl.Buffered(1))
    return pl.pallas_call(
        _in_proj_kernel,
        grid=(TOKENS // tm,),
        in_specs=[row(D_MODEL), lay(1, D_MODEL), lay(D_MODEL, MLA_IN_PAD + DIFF_IN_COLS), lay(1, Q_LORA),
                  lay(Q_LORA, MLA_HEADS * MLA_HEAD_PAD), lay(1, KV_LORA),
                  lay(KV_LORA, MLA_HEADS * (QK_NOPE + V_HEAD))] + [row(LANES)] * 6,
        out_specs=[row(MLA_HEADS * MLA_HEAD_PAD), row(MLA_HEADS * (QK_NOPE + V_HEAD)), row(LANES),
                   row(DIFF_IN_COLS)],
        out_shape=[jax.ShapeDtypeStruct((TOKENS, MLA_HEADS * MLA_HEAD_PAD), BF16),
                   jax.ShapeDtypeStruct((TOKENS, MLA_HEADS * (QK_NOPE + V_HEAD)), BF16),
                   jax.ShapeDtypeStruct((TOKENS, LANES), BF16),
                   jax.ShapeDtypeStruct((TOKENS, DIFF_IN_COLS), BF16)],
        scratch_shapes=[pltpu.VMEM((tm, D_MODEL), BF16)],
        compiler_params=_params(("parallel",)),
        name="in_proj",
    )(h, g_pre, w_in_p, g_q, w_uq, g_kv, w_ukv, *tabs_mla, *tabs_dif)


def _chunk_mask(rows):
    r = lax.broadcasted_iota(jnp.int32, (rows, ATTN_Q_TILE), 0) % ATTN_Q_TILE
    c = lax.broadcasted_iota(jnp.int32, (rows, ATTN_Q_TILE), 1)
    return (c // CHUNK) <= (r // CHUNK)


def _prefix_attention(q, load_k, load_v, qi):
    nt = (((1,), (1,)), ((), ()))
    d0 = qi * ATTN_Q_TILE
    d1 = d0 + ATTN_Q_TILE
    s_d = lax.dot_general(q, load_k(d0, d1), nt, preferred_element_type=F32)
    s_d = jnp.where(_chunk_mask(q.shape[0]), s_d, NEG)
    m = jnp.max(s_d, axis=-1, keepdims=True)
    if qi > 0:
        s_p = lax.dot_general(q, load_k(0, d0), nt, preferred_element_type=F32)
        m = jnp.maximum(m, jnp.max(s_p, axis=-1, keepdims=True))
    pv = _dot(jnp.exp2(s_d - m).astype(BF16), load_v(d0, d1))
    if qi > 0:
        pv = pv + _dot(jnp.exp2(s_p - m).astype(BF16), load_v(0, d0))
    v_dim = pv.shape[1] // 2
    return pv[:, :v_dim] / pv[:, v_dim:]


def _mla_attn_kernel(q_ref, kv_ref, kr_ref, o_ref, kcat_ref, vcat_ref):
    kv_w = QK_NOPE + V_HEAD
    for g in range(ATTN_HEADS):
        kcat_ref[g, :, :QK_NOPE] = kv_ref[:, g * kv_w:g * kv_w + QK_NOPE]
        kcat_ref[g, :, QK_NOPE:] = kr_ref[...]
        vcat_ref[g, :, :V_HEAD] = kv_ref[:, g * kv_w + QK_NOPE:(g + 1) * kv_w]
        vcat_ref[g, :, V_HEAD:] = jnp.ones((SEQ, V_HEAD), BF16)
    for qi in range(SEQ // ATTN_Q_TILE):
        rows = slice(qi * ATTN_Q_TILE, (qi + 1) * ATTN_Q_TILE)
        for g in range(ATTN_HEADS):
            o = _prefix_attention(q_ref[rows, g * MLA_HEAD_PAD:(g + 1) * MLA_HEAD_PAD],
                                  lambda a, b, g=g: kcat_ref[g, a:b, :], lambda a, b, g=g: vcat_ref[g, a:b, :], qi)
            o_ref[rows, g * V_HEAD:(g + 1) * V_HEAD] = o.astype(BF16)


def _mla_attn(q, kv, kr):
    g = ATTN_HEADS
    return pl.pallas_call(
        _mla_attn_kernel,
        grid=(BATCH, MLA_HEADS // g),
        in_specs=[pl.BlockSpec((SEQ, g * MLA_HEAD_PAD), lambda b, h: (b, h)),
                  pl.BlockSpec((SEQ, g * (QK_NOPE + V_HEAD)), lambda b, h: (b, h)),
                  pl.BlockSpec((SEQ, LANES), lambda b, h: (b, 0))],
        out_specs=pl.BlockSpec((SEQ, g * V_HEAD), lambda b, h: (b, h)),
        out_shape=jax.ShapeDtypeStruct((TOKENS, MLA_HEADS * V_HEAD), BF16),
        scratch_shapes=[pltpu.VMEM((g, SEQ, MLA_HEAD_PAD), BF16), pltpu.VMEM((g, SEQ, 2 * V_HEAD), BF16)],
        compiler_params=_params(("parallel", "parallel")),
        name="mla_attn",
    )(q, kv, kr)


def _diff_attn_kernel(lambda_init, q_ref, k_ref, v_ref, lq1_ref, lk1_ref, lq2_ref, lk2_ref, g_ref, o_ref,
                      vcat_ref):
    tq = ATTN_Q_TILE
    hw = 2 * DIFF_D
    for g in range(DIFF_ATTN_HEADS):
        vcat_ref[g, :, :hw] = v_ref[:, g * hw:(g + 1) * hw]
        vcat_ref[g, :, hw:] = jnp.ones((SEQ, hw), BF16)
    lam = (jnp.exp(jnp.sum(lq1_ref[...] * lk1_ref[...], axis=-1, keepdims=True))
           - jnp.exp(jnp.sum(lq2_ref[...] * lk2_ref[...], axis=-1, keepdims=True)) + lambda_init)
    lane = lax.broadcasted_iota(jnp.int32, (tq, LANES), 1)
    zero = jnp.zeros((tq, LANES), BF16)
    for qi in range(SEQ // tq):
        rows = slice(qi * tq, (qi + 1) * tq)
        for g in range(DIFF_ATTN_HEADS):
            cols = slice(g * hw, (g + 1) * hw)
            q = q_ref[rows, cols]
            qq = jnp.concatenate([jnp.where(lane < DIFF_D, q, zero), jnp.where(lane >= DIFF_D, q, zero)], axis=0)
            a = _prefix_attention(qq, lambda a, b, cols=cols: k_ref[a:b, cols],
                                  lambda a, b, g=g: vcat_ref[g, a:b, :], qi)
            o = a[:tq] - lam * a[tq:]
            o_ref[rows, cols] = (_rms(o, g_ref[...]) * (1.0 - lambda_init)).astype(BF16)


def _diff_attn(l, lambda_init, zd, lq1, lk1, lq2, lk2, g_sub):
    g = DIFF_ATTN_HEADS
    hw = 2 * DIFF_D
    steps = DIFF_HEADS // g
    lay = lambda n: pl.BlockSpec((None, 1, n), lambda b, h: (l, 0, 0))
    return pl.pallas_call(
        functools.partial(_diff_attn_kernel, lambda_init),
        grid=(BATCH, steps),
        in_specs=[pl.BlockSpec((SEQ, g * hw), lambda b, h: (b, h)),
                  pl.BlockSpec((SEQ, g * hw), lambda b, h: (b, steps + h)),
                  pl.BlockSpec((SEQ, g * hw), lambda b, h: (b, 2 * steps + h)),
                  lay(DIFF_D), lay(DIFF_D), lay(DIFF_D), lay(DIFF_D), lay(hw)],
        out_specs=pl.BlockSpec((SEQ, g * hw), lambda b, h: (b, h)),
        out_shape=jax.ShapeDtypeStruct((TOKENS, DIFF_V_COLS), BF16),
        scratch_shapes=[pltpu.VMEM((g, SEQ, 2 * hw), BF16)],
        compiler_params=_params(("parallel", "parallel")),
        name="diff_attn",
    )(zd, zd, zd, lq1, lk1, lq2, lk2, g_sub)


def _out_proj_kernel(h_ref, a_ref, d_ref, wa_ref, wd_ref, g_ref, o_ref):
    mix = _dot(a_ref[...], wa_ref[...]) + _dot(d_ref[...], wd_ref[...])
    o_ref[...] = h_ref[...] + _rms(mix, g_ref[...])


def _out_proj(l, h, mla_out, diff_out, w_o, g_post):
    tm = ROW_TILE
    half = MLA_HEADS * V_HEAD
    row = lambda n: pl.BlockSpec((tm, n), lambda i: (i, 0))
    return pl.pallas_call(
        _out_proj_kernel,
        grid=(TOKENS // tm,),
        in_specs=[row(D_MODEL), row(half), row(DIFF_V_COLS),
                  pl.BlockSpec((None, half, D_MODEL), lambda i: (l, 0, 0)),
                  pl.BlockSpec((None, DIFF_V_COLS, D_MODEL), lambda i: (l, 1, 0)),
                  pl.BlockSpec((None, 1, D_MODEL), lambda i: (l, 0, 0))],
        out_specs=row(D_MODEL),
        out_shape=jax.ShapeDtypeStruct((TOKENS, D_MODEL), F32),
        compiler_params=_params(("parallel",)),
        name="out_proj",
    )(h, mla_out, diff_out, w_o, w_o, g_post)


def _ffn_kernel(h_ref, g_pre_ref, wg_ref, wu_ref, cwg_ref, cwu_ref, cbg_ref, cbu_ref, wd_ref, g_post_ref,
                o_ref, xn_ref, ug_ref, uu_ref, act_ref, carry_g_ref, carry_u_ref):
    i, j = pl.program_id(0), pl.program_id(1)
    tm = ROW_TILE
    pad = SUBLANES
    seq_start = i % (SEQ // tm) == 0

    @pl.when(j == 0)
    def _():
        xn_ref[...] = _rms(h_ref[...], g_pre_ref[...]).astype(BF16)
        o_ref[...] = jnp.zeros((tm, D_MODEL), F32)

    @pl.when(jnp.logical_and(i == 0, j == 0))
    def _():
        carry_g_ref[...] = jnp.zeros(carry_g_ref.shape, F32)
        carry_u_ref[...] = jnp.zeros(carry_u_ref.shape, F32)

    def conv_branch(cols, w_ref, cw_ref, cb_ref, u_ref, carry_ref):
        u_ref[pad:, cols] = _dot(xn_ref[...], w_ref[:, cols])
        u_ref[:pad, cols] = jnp.where(seq_start, 0.0, carry_ref[j, :, cols])
        carry_ref[j, :, cols] = u_ref[tm:, cols]
        return (cw_ref[2:3, cols] * u_ref[pad:, cols] + cw_ref[1:2, cols] * u_ref[pad - 1:tm + pad - 1, cols]
                + cw_ref[0:1, cols] * u_ref[pad - 2:tm + pad - 2, cols] + cb_ref[:, cols])

    for c in range(FF_TILE // FF_SUBTILE):
        cols = slice(c * FF_SUBTILE, (c + 1) * FF_SUBTILE)
        gate = conv_branch(cols, wg_ref, cwg_ref, cbg_ref, ug_ref, carry_g_ref)
        up = conv_branch(cols, wu_ref, cwu_ref, cbu_ref, uu_ref, carry_u_ref)
        act_ref[:, cols] = (jax.nn.gelu(gate, approximate=True) * up).astype(BF16)
    o_ref[...] += _dot(act_ref[...], wd_ref[...])

    @pl.when(j == pl.num_programs(1) - 1)
    def _():
        o_ref[...] = h_ref[...] + _rms(o_ref[...], g_post_ref[...])


def _ffn(l, h, g_pre, w_up, conv_w, conv_b, w_down, g_post):
    tm, tf = ROW_TILE, FF_TILE
    nj = D_FF // tf
    row = pl.BlockSpec((tm, D_MODEL), lambda i, j: (i, 0))
    gain = pl.BlockSpec((None, 1, D_MODEL), lambda i, j: (l, 0, 0))
    col = lambda r, off: pl.BlockSpec((None, r, tf), lambda i, j: (l, 0, j + off))
    return pl.pallas_call(
        _ffn_kernel,
        grid=(TOKENS // tm, nj),
        in_specs=[row, gain, col(D_MODEL, 0), col(D_MODEL, nj), col(CONV_WIDTH, 0), col(CONV_WIDTH, nj),
                  col(1, 0), col(1, nj), pl.BlockSpec((None, tf, D_MODEL), lambda i, j: (l, j, 0)), gain],
        out_specs=row,
        out_shape=jax.ShapeDtypeStruct((TOKENS, D_MODEL), F32),
        scratch_shapes=[pltpu.VMEM((tm, D_MODEL), BF16),
                        pltpu.VMEM((tm + SUBLANES, tf), F32), pltpu.VMEM((tm + SUBLANES, tf), F32),
                        pltpu.VMEM((tm, tf), BF16),
                        pltpu.VMEM((nj, SUBLANES, tf), F32), pltpu.VMEM((nj, SUBLANES, tf), F32)],
        compiler_params=_params(("arbitrary", "arbitrary")),
        name="ffn",
    )(h, g_pre, w_up, w_up, conv_w, conv_w, conv_b, conv_b, w_down, g_post)


def _ple_kernel(h_ref, p_ref, w_ple_ref, w_gate_ref, g_ref, o_ref):
    h = h_ref[...]
    gate = jax.nn.sigmoid(_dot(h.astype(BF16), w_gate_ref[...]))
    ple = _dot(p_ref[...].astype(BF16), w_ple_ref[...]) * gate
    o_ref[...] = h + _rms(ple, g_ref[...])


def _ple(l, h, p, w_ple, w_gate, g_ple):
    tm = ROW_TILE
    return pl.pallas_call(
        _ple_kernel,
        grid=(TOKENS // tm,),
        in_specs=[pl.BlockSpec((tm, D_MODEL), lambda i: (i, 0)),
                  pl.BlockSpec((None, tm, PLE_DIM), lambda i: (l, i, 0)),
                  pl.BlockSpec((None, PLE_DIM, D_MODEL), lambda i: (l, 0, 0)),
                  pl.BlockSpec((None, D_MODEL, D_MODEL), lambda i: (l, 0, 0)),
                  pl.BlockSpec((None, 1, D_MODEL), lambda i: (l, 0, 0))],
        out_specs=pl.BlockSpec((tm, D_MODEL), lambda i: (i, 0)),
        out_shape=jax.ShapeDtypeStruct((TOKENS, D_MODEL), F32),
        compiler_params=_params(("parallel",)),
        name="ple",
    )(h, p, w_ple, w_gate, g_ple)


def _rope_tables(positions, rot_dim, period):
    half = rot_dim // 2
    inv = ROPE_THETA ** (-jnp.arange(0, rot_dim, 2, dtype=F32) / rot_dim)
    ang = positions.astype(F32)[..., None] * inv
    cos = jnp.cos(ang).reshape(TOKENS, half)
    sin = jnp.sin(ang).reshape(TOKENS, half)
    rest = period - rot_dim
    ones, zeros, zh = jnp.ones((TOKENS, rest), F32), jnp.zeros((TOKENS, rest), F32), jnp.zeros_like(sin)
    c = jnp.concatenate([cos, cos, ones], axis=1)
    sa = jnp.concatenate([zh, sin, zeros], axis=1)
    sb = jnp.concatenate([-sin, zh, zeros], axis=1)
    reps = LANES // period
    return tuple(jnp.tile(t, (1, reps)) for t in (c, sa, sb))


def kernel(x, p, positions, g_mix_pre, w_in, g_q_lora, w_uq, g_kv_lora, w_ukv, lambda_q1, lambda_k1,
           lambda_q2, lambda_k2, g_diff_sub, w_o, g_mix_post, g_ffn_pre, w_up, conv_w, conv_b, w_down,
           g_ffn_post, w_ple, w_ple_gate, g_ple):
    tabs_mla = _rope_tables(positions, QK_ROPE, LANES)
    tabs_dif = _rope_tables(positions, DIFF_ROT, DIFF_D)

    vec = lambda a: a.reshape(DEPTH, 1, a.shape[-1])
    w_in_b = w_in.astype(BF16)
    w_in_p = jnp.concatenate(
        [w_in_b[:, :, :MLA_IN_COLS], jnp.zeros((DEPTH, D_MODEL, MLA_IN_PAD - MLA_IN_COLS), BF16),
         w_in_b[:, :, MLA_IN_COLS:]], axis=-1)
    uq = w_uq.reshape(DEPTH, Q_LORA, MLA_HEADS, MLA_QK_DIM)
    uq = jnp.pad(uq, ((0, 0), (0, 0), (0, 0), (0, MLA_HEAD_PAD - MLA_QK_DIM)))
    w_uq_p = uq.reshape(DEPTH, Q_LORA, MLA_HEADS * MLA_HEAD_PAD).astype(BF16)
    w_ukv_b = w_ukv.astype(BF16)
    w_o_b = w_o.astype(BF16)
    w_up_b = w_up.astype(BF16)
    w_down_b = w_down.astype(BF16)
    w_ple_b = w_ple.astype(BF16)
    w_gate_b = w_ple_gate.astype(BF16)
    p2 = p.reshape(DEPTH, TOKENS, PLE_DIM)
    g_mix_pre, g_q_lora, g_kv_lora, g_diff_sub, g_mix_post, g_ffn_pre, g_ffn_post, g_ple, conv_b = map(
        vec, (g_mix_pre, g_q_lora, g_kv_lora, g_diff_sub, g_mix_post, g_ffn_pre, g_ffn_post, g_ple, conv_b))
    lambda_q1, lambda_k1, lambda_q2, lambda_k2 = map(vec, (lambda_q1, lambda_k1, lambda_q2, lambda_k2))

    h = x.reshape(TOKENS, D_MODEL)
    for l in range(DEPTH):
        lambda_init = 0.8 - 0.6 * math.exp(-0.3 * l)
        q, kv, kr, zd = _in_proj(l, h, g_mix_pre, w_in_p, g_q_lora, w_uq_p, g_kv_lora, w_ukv_b, tabs_mla, tabs_dif)
        mla_out = _mla_attn(q, kv, kr)
        diff_out = _diff_attn(l, lambda_init, zd, lambda_q1, lambda_k1, lambda_q2, lambda_k2, g_diff_sub)
        h = _out_proj(l, h, mla_out, diff_out, w_o_b, g_mix_post)
        h = _ffn(l, h, g_ffn_pre, w_up_b, conv_w, conv_b, w_down_b, g_ffn_post)
        h = _ple(l, h, p2, w_ple_b, w_gate_b, g_ple)
    return h.reshape(BATCH, SEQ, D_MODEL)
```

```python
import functools
import math

import jax
import jax.numpy as jnp
from jax import lax
from jax.experimental import pallas as pl
from jax.experimental.pallas import tpu as pltpu

D_MODEL = 2048
BATCH = 4
SEQ = 2048
DEPTH = 4
TOKENS = BATCH * SEQ
CHUNK = 64
ROPE_THETA = 500000.0
MLA_HEADS = 8
Q_LORA = 512
KV_LORA = 256
QK_NOPE = 128
QK_ROPE = 64
V_HEAD = 128
MLA_QK_DIM = QK_NOPE + QK_ROPE
DIFF_HEADS = 8
DIFF_D = 64
DIFF_ROT = DIFF_D // 4
DIFF_QK_COLS = DIFF_HEADS * 2 * DIFF_D
DIFF_V_COLS = DIFF_HEADS * 2 * DIFF_D
MLA_IN_COLS = Q_LORA + KV_LORA + QK_ROPE
W_IN_SHIFT = 64
MLA_IN_PAD = MLA_IN_COLS + W_IN_SHIFT
DIFF_IN_COLS = 2 * DIFF_QK_COLS + DIFF_V_COLS
D_FF = 5632
CONV_WIDTH = 3
PLE_DIM = 256
RMS_EPS = 1e-6

LANES = 128
SUBLANES = 8
MXU_COLS = 256
MLA_HEAD_PAD = 256
VMEM_LIMIT = 52 * 1024 * 1024

ROW_TILE = 512
ATTN_Q_TILE = 256
ATTN_HEADS = 4
DIFF_ATTN_HEADS = 2
FF_TILE = 512
FF_SUBTILE = 256

F32 = jnp.float32
BF16 = jnp.bfloat16
NEG = -1e30
LOG2E = math.log2(math.e)


def _rms(x, g):
    return x * lax.rsqrt(jnp.mean(x * x, axis=-1, keepdims=True) + RMS_EPS) * g


def _rope(x, c, sa, sb, half):
    return x * c + pltpu.roll(x, half, 1) * sa + pltpu.roll(x, LANES - half, 1) * sb


def _dot(a, b):
    return jnp.dot(a, b, preferred_element_type=F32)


def _params(sem):
    return pltpu.CompilerParams(dimension_semantics=sem, vmem_limit_bytes=VMEM_LIMIT)


def _in_proj_kernel(h_ref, g_pre_ref, w_in_ref, g_q_ref, w_uq_ref, g_kv_ref, w_ukv_ref,
                    cm_ref, sam_ref, sbm_ref, cd_ref, sad_ref, sbd_ref,
                    q_ref, kv_ref, kr_ref, zd_ref, xn_ref):
    xn_ref[...] = _rms(h_ref[...], g_pre_ref[...]).astype(BF16)

    z = _dot(xn_ref[...], w_in_ref[:, :MLA_IN_PAD])
    tabs = cm_ref[...], sam_ref[...], sbm_ref[...]
    half = QK_ROPE // 2
    scale = MLA_QK_DIM ** -0.5 * LOG2E
    cq = _rms(z[:, :Q_LORA], g_q_ref[...]).astype(BF16)
    for hd in range(MLA_HEADS):
        base = hd * MLA_HEAD_PAD
        qh = _dot(cq, w_uq_ref[:, base:base + MLA_HEAD_PAD])
        q_ref[:, base:base + QK_NOPE] = (qh[:, :QK_NOPE] * scale).astype(BF16)
        q_ref[:, base + QK_NOPE:base + MLA_HEAD_PAD] = (_rope(qh[:, QK_NOPE:], *tabs, half) * scale).astype(BF16)
    ckv = _rms(z[:, Q_LORA:Q_LORA + KV_LORA], g_kv_ref[...]).astype(BF16)
    kv_ref[...] = _dot(ckv, w_ukv_ref[...]).astype(BF16)
    kpe = z[:, Q_LORA + KV_LORA:MLA_IN_COLS]
    kpe = jnp.concatenate([kpe, jnp.zeros_like(kpe)], axis=1)
    kr_ref[...] = _rope(kpe, *tabs, half).astype(BF16)

    k_tabs = cd_ref[...], sad_ref[...], sbd_ref[...]
    q_tabs = tuple(t * (DIFF_D ** -0.5 * LOG2E) for t in k_tabs)
    half = DIFF_ROT // 2
    for t in range(DIFF_IN_COLS // MXU_COLS):
        lo = t * MXU_COLS
        zt = _dot(xn_ref[...], w_in_ref[:, MLA_IN_PAD + lo:MLA_IN_PAD + lo + MXU_COLS])
        if lo >= 2 * DIFF_QK_COLS:
            zd_ref[:, lo:lo + MXU_COLS] = zt.astype(BF16)
        else:
            tabs = q_tabs if lo < DIFF_QK_COLS else k_tabs
            for u in range(MXU_COLS // LANES):
                zd_ref[:, lo + u * LANES:lo + (u + 1) * LANES] = _rope(
                    zt[:, u * LANES:(u + 1) * LANES], *tabs, half).astype(BF16)


def _in_proj(l, h, g_pre, w_in_p, g_q, w_uq, g_kv, w_ukv, tabs_mla, tabs_dif):
    tm = ROW_TILE
    row = lambda n: pl.BlockSpec((tm, n), lambda i: (i, 0))
    lay = lambda *s: pl.BlockSpec((None,) + s, lambda i: (l,) + (0,) * len(s), pipeline_mode=pl.Buffered(1))
    return pl.pallas_call(
        _in_proj_kernel,
        grid=(TOKENS // tm,),
        in_specs=[row(D_MODEL), lay(1, D_MODEL), lay(D_MODEL, MLA_IN_PAD + DIFF_IN_COLS), lay(1, Q_LORA),
                  lay(Q_LORA, MLA_HEADS * MLA_HEAD_PAD), lay(1, KV_LORA),
                  lay(KV_LORA, MLA_HEADS * (QK_NOPE + V_HEAD))] + [row(LANES)] * 6,
        out_specs=[row(MLA_HEADS * MLA_HEAD_PAD), row(MLA_HEADS * (QK_NOPE + V_HEAD)), row(LANES),
                   row(DIFF_IN_COLS)],
        out_shape=[jax.ShapeDtypeStruct((TOKENS, MLA_HEADS * MLA_HEAD_PAD), BF16),
                   jax.ShapeDtypeStruct((TOKENS, MLA_HEADS * (QK_NOPE + V_HEAD)), BF16),
                   jax.ShapeDtypeStruct((TOKENS, LANES), BF16),
                   jax.ShapeDtypeStruct((TOKENS, DIFF_IN_COLS), BF16)],
        scratch_shapes=[pltpu.VMEM((tm, D_MODEL), BF16)],
        compiler_params=_params(("parallel",)),
        name="in_proj",
    )(h, g_pre, w_in_p, g_q, w_uq, g_kv, w_ukv, *tabs_mla, *tabs_dif)


def _chunk_mask(rows):
    r = lax.broadcasted_iota(jnp.int32, (rows, ATTN_Q_TILE), 0) % ATTN_Q_TILE
    c = lax.broadcasted_iota(jnp.int32, (rows, ATTN_Q_TILE), 1)
    return (c // CHUNK) <= (r // CHUNK)


def _prefix_attention(q, load_k, load_v, qi):
    nt = (((1,), (1,)), ((), ()))
    d0 = qi * ATTN_Q_TILE
    d1 = d0 + ATTN_Q_TILE
    s_d = lax.dot_general(q, load_k(d0, d1), nt, preferred_element_type=F32)
    s_d = jnp.where(_chunk_mask(q.shape[0]), s_d, NEG)
    m = jnp.max(s_d, axis=-1, keepdims=True)
    if qi > 0:
        s_p = lax.dot_general(q, load_k(0, d0), nt, preferred_element_type=F32)
        m = jnp.maximum(m, jnp.max(s_p, axis=-1, keepdims=True))
    pv = _dot(jnp.exp2(s_d - m).astype(BF16), load_v(d0, d1))
    if qi > 0:
        pv = pv + _dot(jnp.exp2(s_p - m).astype(BF16), load_v(0, d0))
    v_dim = pv.shape[1] // 2
    return pv[:, :v_dim] / pv[:, v_dim:]


def _mla_attn_kernel(q_ref, kv_ref, kr_ref, o_ref, kcat_ref, vcat_ref):
    kv_w = QK_NOPE + V_HEAD
    for g in range(ATTN_HEADS):
        kcat_ref[g, :, :QK_NOPE] = kv_ref[:, g * kv_w:g * kv_w + QK_NOPE]
        kcat_ref[g, :, QK_NOPE:] = kr_ref[...]
        vcat_ref[g, :, :V_HEAD] = kv_ref[:, g * kv_w + QK_NOPE:(g + 1) * kv_w]
        vcat_ref[g, :, V_HEAD:] = jnp.ones((SEQ, V_HEAD), BF16)
    for qi in range(SEQ // ATTN_Q_TILE):
        rows = slice(qi * ATTN_Q_TILE, (qi + 1) * ATTN_Q_TILE)
        for g in range(ATTN_HEADS):
            o = _prefix_attention(q_ref[rows, g * MLA_HEAD_PAD:(g + 1) * MLA_HEAD_PAD],
                                  lambda a, b, g=g: kcat_ref[g, a:b, :], lambda a, b, g=g: vcat_ref[g, a:b, :], qi)
            o_ref[rows, g * V_HEAD:(g + 1) * V_HEAD] = o.astype(BF16)


def _mla_attn(q, kv, kr):
    g = ATTN_HEADS
    return pl.pallas_call(
        _mla_attn_kernel,
        grid=(BATCH, MLA_HEADS // g),
        in_specs=[pl.BlockSpec((SEQ, g * MLA_HEAD_PAD), lambda b, h: (b, h)),
                  pl.BlockSpec((SEQ, g * (QK_NOPE + V_HEAD)), lambda b, h: (b, h)),
                  pl.BlockSpec((SEQ, LANES), lambda b, h: (b, 0))],
        out_specs=pl.BlockSpec((SEQ, g * V_HEAD), lambda b, h: (b, h)),
        out_shape=jax.ShapeDtypeStruct((TOKENS, MLA_HEADS * V_HEAD), BF16),
        scratch_shapes=[pltpu.VMEM((g, SEQ, MLA_HEAD_PAD), BF16), pltpu.VMEM((g, SEQ, 2 * V_HEAD), BF16)],
        compiler_params=_params(("parallel", "parallel")),
        name="mla_attn",
    )(q, kv, kr)


def _diff_attn_kernel(lambda_init, q_ref, k_ref, v_ref, lq1_ref, lk1_ref, lq2_ref, lk2_ref, g_ref, o_ref,
                      vcat_ref):
    tq = ATTN_Q_TILE
    hw = 2 * DIFF_D
    for g in range(DIFF_ATTN_HEADS):
        vcat_ref[g, :, :hw] = v_ref[:, g * hw:(g + 1) * hw]
        vcat_ref[g, :, hw:] = jnp.ones((SEQ, hw), BF16)
    lam = (jnp.exp(jnp.sum(lq1_ref[...] * lk1_ref[...], axis=-1, keepdims=True))
           - jnp.exp(jnp.sum(lq2_ref[...] * lk2_ref[...], axis=-1, keepdims=True)) + lambda_init)
    lane = lax.broadcasted_iota(jnp.int32, (tq, LANES), 1)
    zero = jnp.zeros((tq, LANES), BF16)
    for qi in range(SEQ // tq):
        rows = slice(qi * tq, (qi + 1) * tq)
        for g in range(DIFF_ATTN_HEADS):
            cols = slice(g * hw, (g + 1) * hw)
            q = q_ref[rows, cols]
            qq = jnp.concatenate([jnp.where(lane < DIFF_D, q, zero), jnp.where(lane >= DIFF_D, q, zero)], axis=0)
            a = _prefix_attention(qq, lambda a, b, cols=cols: k_ref[a:b, cols],
                                  lambda a, b, g=g: vcat_ref[g, a:b, :], qi)
            o = a[:tq] - lam * a[tq:]
            o_ref[rows, cols] = (_rms(o, g_ref[...]) * (1.0 - lambda_init)).astype(BF16)


def _diff_attn(l, lambda_init, zd, lq1, lk1, lq2, lk2, g_sub):
    g = DIFF_ATTN_HEADS
    hw = 2 * DIFF_D
    steps = DIFF_HEADS // g
    lay = lambda n: pl.BlockSpec((None, 1, n), lambda b, h: (l, 0, 0))
    return pl.pallas_call(
        functools.partial(_diff_attn_kernel, lambda_init),
        grid=(BATCH, steps),
        in_specs=[pl.BlockSpec((SEQ, g * hw), lambda b, h: (b, h)),
                  pl.BlockSpec((SEQ, g * hw), lambda b, h: (b, steps + h)),
                  pl.BlockSpec((SEQ, g * hw), lambda b, h: (b, 2 * steps + h)),
                  lay(DIFF_D), lay(DIFF_D), lay(DIFF_D), lay(DIFF_D), lay(hw)],
        out_specs=pl.BlockSpec((SEQ, g * hw), lambda b, h: (b, h)),
        out_shape=jax.ShapeDtypeStruct((TOKENS, DIFF_V_COLS), BF16),
        scratch_shapes=[pltpu.VMEM((g, SEQ, 2 * hw), BF16)],
        compiler_params=_params(("parallel", "parallel")),
        name="diff_attn",
    )(zd, zd, zd, lq1, lk1, lq2, lk2, g_sub)


def _out_proj_kernel(h_ref, a_ref, d_ref, wa_ref, wd_ref, g_ref, o_ref):
    mix = _dot(a_ref[...], wa_ref[...]) + _dot(d_ref[...], wd_ref[...])
    o_ref[...] = h_ref[...] + _rms(mix, g_ref[...])


def _out_proj(l, h, mla_out, diff_out, w_o, g_post):
    tm = ROW_TILE
    half = MLA_HEADS * V_HEAD
    row = lambda n: pl.BlockSpec((tm, n), lambda i: (i, 0))
    return pl.pallas_call(
        _out_proj_kernel,
        grid=(TOKENS // tm,),
        in_specs=[row(D_MODEL), row(half), row(DIFF_V_COLS),
                  pl.BlockSpec((None, half, D_MODEL), lambda i: (l, 0, 0)),
                  pl.BlockSpec((None, DIFF_V_COLS, D_MODEL), lambda i: (l, 1, 0)),
                  pl.BlockSpec((None, 1, D_MODEL), lambda i: (l, 0, 0))],
        out_specs=row(D_MODEL),
        out_shape=jax.ShapeDtypeStruct((TOKENS, D_MODEL), F32),
        compiler_params=_params(("parallel",)),
        name="out_proj",
    )(h, mla_out, diff_out, w_o, w_o, g_post)


def _ffn_kernel(h_ref, g_pre_ref, wg_ref, wu_ref, cwg_ref, cwu_ref, cbg_ref, cbu_ref, wd_ref, g_post_ref,
                o_ref, xn_ref, ug_ref, uu_ref, act_ref, carry_g_ref, carry_u_ref):
    i, j = pl.program_id(0), pl.program_id(1)
    tm = ROW_TILE
    pad = SUBLANES
    seq_start = i % (SEQ // tm) == 0

    @pl.when(j == 0)
    def _():
        xn_ref[...] = _rms(h_ref[...], g_pre_ref[...]).astype(BF16)
        o_ref[...] = jnp.zeros((tm, D_MODEL), F32)

    @pl.when(jnp.logical_and(i == 0, j == 0))
    def _():
        carry_g_ref[...] = jnp.zeros(carry_g_ref.shape, F32)
        carry_u_ref[...] = jnp.zeros(carry_u_ref.shape, F32)

    def conv_branch(cols, w_ref, cw_ref, cb_ref, u_ref, carry_ref):
        u_ref[pad:, cols] = _dot(xn_ref[...], w_ref[:, cols])
        u_ref[:pad, cols] = jnp.where(seq_start, 0.0, carry_ref[j, :, cols])
        carry_ref[j, :, cols] = u_ref[tm:, cols]
        return (cw_ref[2:3, cols] * u_ref[pad:, cols] + cw_ref[1:2, cols] * u_ref[pad - 1:tm + pad - 1, cols]
                + cw_ref[0:1, cols] * u_ref[pad - 2:tm + pad - 2, cols] + cb_ref[:, cols])

    for c in range(FF_TILE // FF_SUBTILE):
        cols = slice(c * FF_SUBTILE, (c + 1) * FF_SUBTILE)
        gate = conv_branch(cols, wg_ref, cwg_ref, cbg_ref, ug_ref, carry_g_ref)
        up = conv_branch(cols, wu_ref, cwu_ref, cbu_ref, uu_ref, carry_u_ref)
        act_ref[:, cols] = (jax.nn.gelu(gate, approximate=True) * up).astype(BF16)
    o_ref[...] += _dot(act_ref[...], wd_ref[...])

    @pl.when(j == pl.num_programs(1) - 1)
    def _():
        o_ref[...] = h_ref[...] + _rms(o_ref[...], g_post_ref[...])


def _ffn(l, h, g_pre, w_up, conv_w, conv_b, w_down, g_post):
    tm, tf = ROW_TILE, FF_TILE
    nj = D_FF // tf
    row = pl.BlockSpec((tm, D_MODEL), lambda i, j: (i, 0))
    gain = pl.BlockSpec((None, 1, D_MODEL), lambda i, j: (l, 0, 0))
    col = lambda r, off: pl.BlockSpec((None, r, tf), lambda i, j: (l, 0, j + off))
    return pl.pallas_call(
        _ffn_kernel,
        grid=(TOKENS // tm, nj),
        in_specs=[row, gain, col(D_MODEL, 0), col(D_MODEL, nj), col(CONV_WIDTH, 0), col(CONV_WIDTH, nj),
                  col(1, 0), col(1, nj), pl.BlockSpec((None, tf, D_MODEL), lambda i, j: (l, j, 0)), gain],
        out_specs=row,
        out_shape=jax.ShapeDtypeStruct((TOKENS, D_MODEL), F32),
        scratch_shapes=[pltpu.VMEM((tm, D_MODEL), BF16),
                        pltpu.VMEM((tm + SUBLANES, tf), F32), pltpu.VMEM((tm + SUBLANES, tf), F32),
                        pltpu.VMEM((tm, tf), BF16),
                        pltpu.VMEM((nj, SUBLANES, tf), F32), pltpu.VMEM((nj, SUBLANES, tf), F32)],
        compiler_params=_params(("arbitrary", "arbitrary")),
        name="ffn",
    )(h, g_pre, w_up, w_up, conv_w, conv_w, conv_b, conv_b, w_down, g_post)


def _ple_kernel(h_ref, p_ref, w_ple_ref, w_gate_ref, g_ref, o_ref):
    h = h_ref[...]
    gate = jax.nn.sigmoid(_dot(h.astype(BF16), w_gate_ref[...]))
    ple = _dot(p_ref[...].astype(BF16), w_ple_ref[...]) * gate
    o_ref[...] = h + _rms(ple, g_ref[...])


def _ple(l, h, p, w_ple, w_gate, g_ple):
    tm = ROW_TILE
    return pl.pallas_call(
        _ple_kernel,
        grid=(TOKENS // tm,),
        in_specs=[pl.BlockSpec((tm, D_MODEL), lambda i: (i, 0)),
                  pl.BlockSpec((None, tm, PLE_DIM), lambda i: (l, i, 0)),
                  pl.BlockSpec((None, PLE_DIM, D_MODEL), lambda i: (l, 0, 0)),
                  pl.BlockSpec((None, D_MODEL, D_MODEL), lambda i: (l, 0, 0)),
                  pl.BlockSpec((None, 1, D_MODEL), lambda i: (l, 0, 0))],
        out_specs=pl.BlockSpec((tm, D_MODEL), lambda i: (i, 0)),
        out_shape=jax.ShapeDtypeStruct((TOKENS, D_MODEL), F32),
        compiler_params=_params(("parallel",)),
        name="ple",
    )(h, p, w_ple, w_gate, g_ple)


def _rope_tables(positions, rot_dim, period):
    half = rot_dim // 2
    inv = ROPE_THETA ** (-jnp.arange(0, rot_dim, 2, dtype=F32) / rot_dim)
    ang = positions.astype(F32)[..., None] * inv
    cos = jnp.cos(ang).reshape(TOKENS, half)
    sin = jnp.sin(ang).reshape(TOKENS, half)
    rest = period - rot_dim
    ones, zeros, zh = jnp.ones((TOKENS, rest), F32), jnp.zeros((TOKENS, rest), F32), jnp.zeros_like(sin)
    c = jnp.concatenate([cos, cos, ones], axis=1)
    sa = jnp.concatenate([zh, sin, zeros], axis=1)
    sb = jnp.concatenate([-sin, zh, zeros], axis=1)
    reps = LANES // period
    return tuple(jnp.tile(t, (1, reps)) for t in (c, sa, sb))


def kernel(x, p, positions, g_mix_pre, w_in, g_q_lora, w_uq, g_kv_lora, w_ukv, lambda_q1, lambda_k1,
           lambda_q2, lambda_k2, g_diff_sub, w_o, g_mix_post, g_ffn_pre, w_up, conv_w, conv_b, w_down,
           g_ffn_post, w_ple, w_ple_gate, g_ple):
    tabs_mla = _rope_tables(positions, QK_ROPE, LANES)
    tabs_dif = _rope_tables(positions, DIFF_ROT, DIFF_D)

    vec = lambda a: a.reshape(DEPTH, 1, a.shape[-1])
    col = lax.broadcasted_iota(jnp.int32, (1, 1, MLA_IN_PAD + DIFF_IN_COLS), 2)
    w_in_p = jnp.where(col < MLA_IN_COLS, jnp.pad(w_in, ((0, 0), (0, 0), (0, W_IN_SHIFT))),
                       jnp.pad(w_in, ((0, 0), (0, 0), (W_IN_SHIFT, 0)))).astype(BF16)
    uq = w_uq.reshape(DEPTH, Q_LORA, MLA_HEADS, MLA_QK_DIM)
    uq = jnp.pad(uq, ((0, 0), (0, 0), (0, 0), (0, MLA_HEAD_PAD - MLA_QK_DIM)))
    w_uq_p = uq.reshape(DEPTH, Q_LORA, MLA_HEADS * MLA_HEAD_PAD).astype(BF16)
    w_ukv_b = w_ukv.astype(BF16)
    w_o_b = w_o.astype(BF16)
    w_up_b = w_up.astype(BF16)
    w_down_b = w_down.astype(BF16)
    w_ple_b = w_ple.astype(BF16)
    w_gate_b = w_ple_gate.astype(BF16)
    p2 = p.reshape(DEPTH, TOKENS, PLE_DIM)
    g_mix_pre, g_q_lora, g_kv_lora, g_diff_sub, g_mix_post, g_ffn_pre, g_ffn_post, g_ple, conv_b = map(
        vec, (g_mix_pre, g_q_lora, g_kv_lora, g_diff_sub, g_mix_post, g_ffn_pre, g_ffn_post, g_ple, conv_b))
    lambda_q1, lambda_k1, lambda_q2, lambda_k2 = map(vec, (lambda_q1, lambda_k1, lambda_q2, lambda_k2))

    h = x.reshape(TOKENS, D_MODEL)
    for l in range(DEPTH):
        lambda_init = 0.8 - 0.6 * math.exp(-0.3 * l)
        q, kv, kr, zd = _in_proj(l, h, g_mix_pre, w_in_p, g_q_lora, w_uq_p, g_kv_lora, w_ukv_b, tabs_mla, tabs_dif)
        mla_out = _mla_attn(q, kv, kr)
        diff_out = _diff_attn(l, lambda_init, zd, lambda_q1, lambda_k1, lambda_q2, lambda_k2, g_diff_sub)
        h = _out_proj(l, h, mla_out, diff_out, w_o_b, g_mix_post)
        h = _ffn(l, h, g_ffn_pre, w_up_b, conv_w, conv_b, w_down_b, g_ffn_post)
        h = _ple(l, h, p2, w_ple_b, w_gate_b, g_ple)
    return h.reshape(BATCH, SEQ, D_MODEL)
```

```python
import functools
import math

import jax
import jax.numpy as jnp
from jax import lax
from jax.experimental import pallas as pl
from jax.experimental.pallas import tpu as pltpu

D_MODEL = 2048
BATCH = 4
SEQ = 2048
DEPTH = 4
TOKENS = BATCH * SEQ
CHUNK = 64
ROPE_THETA = 500000.0
MLA_HEADS = 8
Q_LORA = 512
KV_LORA = 256
QK_NOPE = 128
QK_ROPE = 64
V_HEAD = 128
MLA_QK_DIM = QK_NOPE + QK_ROPE
DIFF_HEADS = 8
DIFF_D = 64
DIFF_ROT = DIFF_D // 4
DIFF_QK_COLS = DIFF_HEADS * 2 * DIFF_D
DIFF_V_COLS = DIFF_HEADS * 2 * DIFF_D
MLA_IN_COLS = Q_LORA + KV_LORA + QK_ROPE
DIFF_IN_COLS = 2 * DIFF_QK_COLS + DIFF_V_COLS
D_FF = 5632
CONV_WIDTH = 3
PLE_DIM = 256
RMS_EPS = 1e-6

LANES = 128
SUBLANES = 8
MXU_COLS = 256
MLA_HEAD_PAD = 256
VMEM_LIMIT = 52 * 1024 * 1024

ROW_TILE = 512
ATTN_Q_TILE = 256
ATTN_HEADS = 4
DIFF_ATTN_HEADS = 4
FF_TILE = 512
FF_SUBTILE = 256

F32 = jnp.float32
BF16 = jnp.bfloat16
NEG = -1e30
LOG2E = math.log2(math.e)


def _rms(x, g):
    return x * lax.rsqrt(jnp.mean(x * x, axis=-1, keepdims=True) + RMS_EPS) * g


def _rope(x, c, sa, sb, half):
    return x * c + pltpu.roll(x, half, 1) * sa + pltpu.roll(x, LANES - half, 1) * sb


def _dot(a, b):
    return jnp.dot(a, b, preferred_element_type=F32)


def _dot_nt(a, b):
    return lax.dot_general(a, b, (((1,), (1,)), ((), ())), preferred_element_type=F32)


def _params(sem):
    return pltpu.CompilerParams(dimension_semantics=sem, vmem_limit_bytes=VMEM_LIMIT)


def _in_proj_kernel(h_ref, g_pre_ref, w_in_ref, g_q_ref, w_uq_ref, g_kv_ref, w_ukv_ref,
                    cm_ref, sam_ref, sbm_ref, cd_ref, sad_ref, sbd_ref,
                    q_ref, kv_ref, kr_ref, zd_ref, xn_ref):
    xn_ref[...] = _rms(h_ref[...], g_pre_ref[...]).astype(BF16)

    z = _dot_nt(xn_ref[...], w_in_ref[:MLA_IN_COLS, :])
    tabs = cm_ref[...], sam_ref[...], sbm_ref[...]
    half = QK_ROPE // 2
    scale = MLA_QK_DIM ** -0.5 * LOG2E
    cq = _rms(z[:, :Q_LORA], g_q_ref[...]).astype(BF16)
    for hd in range(MLA_HEADS):
        base = hd * MLA_HEAD_PAD
        qh = _dot(cq, w_uq_ref[:, base:base + MLA_HEAD_PAD])
        q_ref[:, base:base + QK_NOPE] = (qh[:, :QK_NOPE] * scale).astype(BF16)
        q_ref[:, base + QK_NOPE:base + MLA_HEAD_PAD] = (_rope(qh[:, QK_NOPE:], *tabs, half) * scale).astype(BF16)
    ckv = _rms(z[:, Q_LORA:Q_LORA + KV_LORA], g_kv_ref[...]).astype(BF16)
    kv_ref[...] = _dot(ckv, w_ukv_ref[...]).astype(BF16)
    kpe = z[:, Q_LORA + KV_LORA:MLA_IN_COLS]
    kpe = jnp.concatenate([kpe, jnp.zeros_like(kpe)], axis=1)
    kr_ref[...] = _rope(kpe, *tabs, half).astype(BF16)

    k_tabs = cd_ref[...], sad_ref[...], sbd_ref[...]
    q_tabs = tuple(t * (DIFF_D ** -0.5 * LOG2E) for t in k_tabs)
    half = DIFF_ROT // 2
    for t in range(DIFF_IN_COLS // MXU_COLS):
        lo = t * MXU_COLS
        zt = _dot_nt(xn_ref[...], w_in_ref[MLA_IN_COLS + lo:MLA_IN_COLS + lo + MXU_COLS, :])
        if lo >= 2 * DIFF_QK_COLS:
            zd_ref[:, lo:lo + MXU_COLS] = zt.astype(BF16)
        else:
            tabs = q_tabs if lo < DIFF_QK_COLS else k_tabs
            for u in range(MXU_COLS // LANES):
                zd_ref[:, lo + u * LANES:lo + (u + 1) * LANES] = _rope(
                    zt[:, u * LANES:(u + 1) * LANES], *tabs, half).astype(BF16)


def _in_proj(l, h, g_pre, w_in_p, g_q, w_uq, g_kv, w_ukv, tabs_mla, tabs_dif):
    tm = ROW_TILE
    row = lambda n: pl.BlockSpec((tm, n), lambda i: (i, 0))
    lay = lambda *s: pl.BlockSpec((None,) + s, lambda i: (l,) + (0,) * len(s), pipeline_mode=pl.Buffered(1))
    return pl.pallas_call(
        _in_proj_kernel,
        grid=(TOKENS // tm,),
        in_specs=[row(D_MODEL), lay(1, D_MODEL), lay(MLA_IN_COLS + DIFF_IN_COLS, D_MODEL), lay(1, Q_LORA),
                  lay(Q_LORA, MLA_HEADS * MLA_HEAD_PAD), lay(1, KV_LORA),
                  lay(KV_LORA, MLA_HEADS * (QK_NOPE + V_HEAD))] + [row(LANES)] * 6,
        out_specs=[row(MLA_HEADS * MLA_HEAD_PAD), row(MLA_HEADS * (QK_NOPE + V_HEAD)), row(LANES),
                   row(DIFF_IN_COLS)],
        out_shape=[jax.ShapeDtypeStruct((TOKENS, MLA_HEADS * MLA_HEAD_PAD), BF16),
                   jax.ShapeDtypeStruct((TOKENS, MLA_HEADS * (QK_NOPE + V_HEAD)), BF16),
                   jax.ShapeDtypeStruct((TOKENS, LANES), BF16),
                   jax.ShapeDtypeStruct((TOKENS, DIFF_IN_COLS), BF16)],
        scratch_shapes=[pltpu.VMEM((tm, D_MODEL), BF16)],
        compiler_params=_params(("parallel",)),
        name="in_proj",
    )(h, g_pre, w_in_p, g_q, w_uq, g_kv, w_ukv, *tabs_mla, *tabs_dif)


def _chunk_mask(rows):
    r = lax.broadcasted_iota(jnp.int32, (rows, ATTN_Q_TILE), 0) % ATTN_Q_TILE
    c = lax.broadcasted_iota(jnp.int32, (rows, ATTN_Q_TILE), 1)
    return (c // CHUNK) <= (r // CHUNK)


def _prefix_attention(q, load_k, load_v, qi):
    nt = (((1,), (1,)), ((), ()))
    d0 = qi * ATTN_Q_TILE
    d1 = d0 + ATTN_Q_TILE
    s_d = lax.dot_general(q, load_k(d0, d1), nt, preferred_element_type=F32)
    s_d = jnp.where(_chunk_mask(q.shape[0]), s_d, NEG)
    m = jnp.max(s_d, axis=-1, keepdims=True)
    if qi > 0:
        s_p = lax.dot_general(q, load_k(0, d0), nt, preferred_element_type=F32)
        m = jnp.maximum(m, jnp.max(s_p, axis=-1, keepdims=True))
    pv = _dot(jnp.exp2(s_d - m).astype(BF16), load_v(d0, d1))
    if qi > 0:
        pv = pv + _dot(jnp.exp2(s_p - m).astype(BF16), load_v(0, d0))
    v_dim = pv.shape[1] // 2
    return pv[:, :v_dim] / pv[:, v_dim:]


def _mla_attn_kernel(q_ref, kv_ref, kr_ref, o_ref, kcat_ref, vcat_ref):
    kv_w = QK_NOPE + V_HEAD
    for g in range(ATTN_HEADS):
        kcat_ref[g, :, :QK_NOPE] = kv_ref[:, g * kv_w:g * kv_w + QK_NOPE]
        kcat_ref[g, :, QK_NOPE:] = kr_ref[...]
        vcat_ref[g, :, :V_HEAD] = kv_ref[:, g * kv_w + QK_NOPE:(g + 1) * kv_w]
        vcat_ref[g, :, V_HEAD:] = jnp.ones((SEQ, V_HEAD), BF16)
    for qi in range(SEQ // ATTN_Q_TILE):
        rows = slice(qi * ATTN_Q_TILE, (qi + 1) * ATTN_Q_TILE)
        for g in range(ATTN_HEADS):
            o = _prefix_attention(q_ref[rows, g * MLA_HEAD_PAD:(g + 1) * MLA_HEAD_PAD],
                                  lambda a, b, g=g: kcat_ref[g, a:b, :], lambda a, b, g=g: vcat_ref[g, a:b, :], qi)
            o_ref[rows, g * V_HEAD:(g + 1) * V_HEAD] = o.astype(BF16)


def _mla_attn(q, kv, kr):
    g = ATTN_HEADS
    return pl.pallas_call(
        _mla_attn_kernel,
        grid=(BATCH, MLA_HEADS // g),
        in_specs=[pl.BlockSpec((SEQ, g * MLA_HEAD_PAD), lambda b, h: (b, h)),
                  pl.BlockSpec((SEQ, g * (QK_NOPE + V_HEAD)), lambda b, h: (b, h)),
                  pl.BlockSpec((SEQ, LANES), lambda b, h: (b, 0))],
        out_specs=pl.BlockSpec((SEQ, g * V_HEAD), lambda b, h: (b, h)),
        out_shape=jax.ShapeDtypeStruct((TOKENS, MLA_HEADS * V_HEAD), BF16),
        scratch_shapes=[pltpu.VMEM((g, SEQ, MLA_HEAD_PAD), BF16), pltpu.VMEM((g, SEQ, 2 * V_HEAD), BF16)],
        compiler_params=_params(("parallel", "parallel")),
        name="mla_attn",
    )(q, kv, kr)


def _diff_attn_kernel(lambda_init, q_ref, k_ref, v_ref, lq1_ref, lk1_ref, lq2_ref, lk2_ref, g_ref, o_ref,
                      vcat_ref):
    tq = ATTN_Q_TILE
    hw = 2 * DIFF_D
    for g in range(DIFF_ATTN_HEADS):
        vcat_ref[g, :, :hw] = v_ref[:, g * hw:(g + 1) * hw]
        vcat_ref[g, :, hw:] = jnp.ones((SEQ, hw), BF16)
    lam = (jnp.exp(jnp.sum(lq1_ref[...] * lk1_ref[...], axis=-1, keepdims=True))
           - jnp.exp(jnp.sum(lq2_ref[...] * lk2_ref[...], axis=-1, keepdims=True)) + lambda_init)
    lane = lax.broadcasted_iota(jnp.int32, (tq, LANES), 1)
    zero = jnp.zeros((tq, LANES), BF16)
    for qi in range(SEQ // tq):
        rows = slice(qi * tq, (qi + 1) * tq)
        for g in range(DIFF_ATTN_HEADS):
            cols = slice(g * hw, (g + 1) * hw)
            q = q_ref[rows, cols]
            qq = jnp.concatenate([jnp.where(lane < DIFF_D, q, zero), jnp.where(lane >= DIFF_D, q, zero)], axis=0)
            a = _prefix_attention(qq, lambda a, b, cols=cols: k_ref[a:b, cols],
                                  lambda a, b, g=g: vcat_ref[g, a:b, :], qi)
            o = a[:tq] - lam * a[tq:]
            o_ref[rows, cols] = (_rms(o, g_ref[...]) * (1.0 - lambda_init)).astype(BF16)


def _diff_attn(l, lambda_init, zd, lq1, lk1, lq2, lk2, g_sub):
    g = DIFF_ATTN_HEADS
    hw = 2 * DIFF_D
    steps = DIFF_HEADS // g
    lay = lambda n: pl.BlockSpec((None, 1, n), lambda b, h: (l, 0, 0))
    return pl.pallas_call(
        functools.partial(_diff_attn_kernel, lambda_init),
        grid=(BATCH, steps),
        in_specs=[pl.BlockSpec((SEQ, g * hw), lambda b, h: (b, h)),
                  pl.BlockSpec((SEQ, g * hw), lambda b, h: (b, steps + h)),
                  pl.BlockSpec((SEQ, g * hw), lambda b, h: (b, 2 * steps + h)),
                  lay(DIFF_D), lay(DIFF_D), lay(DIFF_D), lay(DIFF_D), lay(hw)],
        out_specs=pl.BlockSpec((SEQ, g * hw), lambda b, h: (b, h)),
        out_shape=jax.ShapeDtypeStruct((TOKENS, DIFF_V_COLS), BF16),
        scratch_shapes=[pltpu.VMEM((g, SEQ, 2 * hw), BF16)],
        compiler_params=_params(("parallel", "parallel")),
        name="diff_attn",
    )(zd, zd, zd, lq1, lk1, lq2, lk2, g_sub)


def _out_proj_kernel(h_ref, a_ref, d_ref, wa_ref, wd_ref, g_ref, o_ref):
    mix = _dot(a_ref[...], wa_ref[...]) + _dot(d_ref[...], wd_ref[...])
    o_ref[...] = h_ref[...] + _rms(mix, g_ref[...])


def _out_proj(l, h, mla_out, diff_out, w_o, g_post):
    tm = ROW_TILE
    half = MLA_HEADS * V_HEAD
    row = lambda n: pl.BlockSpec((tm, n), lambda i: (i, 0))
    return pl.pallas_call(
        _out_proj_kernel,
        grid=(TOKENS // tm,),
        in_specs=[row(D_MODEL), row(half), row(DIFF_V_COLS),
                  pl.BlockSpec((None, half, D_MODEL), lambda i: (l, 0, 0)),
                  pl.BlockSpec((None, DIFF_V_COLS, D_MODEL), lambda i: (l, 1, 0)),
                  pl.BlockSpec((None, 1, D_MODEL), lambda i: (l, 0, 0))],
        out_specs=row(D_MODEL),
        out_shape=jax.ShapeDtypeStruct((TOKENS, D_MODEL), F32),
        compiler_params=_params(("parallel",)),
        name="out_proj",
    )(h, mla_out, diff_out, w_o, w_o, g_post)


def _ffn_kernel(h_ref, g_pre_ref, wg_ref, wu_ref, cwg_ref, cwu_ref, cbg_ref, cbu_ref, wd_ref, g_post_ref,
                o_ref, xn_ref, ug_ref, uu_ref, act_ref, carry_g_ref, carry_u_ref):
    i, j = pl.program_id(0), pl.program_id(1)
    tm = ROW_TILE
    pad = SUBLANES
    seq_start = i % (SEQ // tm) == 0

    @pl.when(j == 0)
    def _():
        xn_ref[...] = _rms(h_ref[...], g_pre_ref[...]).astype(BF16)
        o_ref[...] = jnp.zeros((tm, D_MODEL), F32)

    @pl.when(jnp.logical_and(i == 0, j == 0))
    def _():
        carry_g_ref[...] = jnp.zeros(carry_g_ref.shape, F32)
        carry_u_ref[...] = jnp.zeros(carry_u_ref.shape, F32)

    def conv_branch(cols, w_ref, cw_ref, cb_ref, u_ref, carry_ref):
        u_ref[pad:, cols] = _dot(xn_ref[...], w_ref[:, cols])
        u_ref[:pad, cols] = jnp.where(seq_start, 0.0, carry_ref[j, :, cols])
        carry_ref[j, :, cols] = u_ref[tm:, cols]
        return (cw_ref[2:3, cols] * u_ref[pad:, cols] + cw_ref[1:2, cols] * u_ref[pad - 1:tm + pad - 1, cols]
                + cw_ref[0:1, cols] * u_ref[pad - 2:tm + pad - 2, cols] + cb_ref[:, cols])

    for c in range(FF_TILE // FF_SUBTILE):
        cols = slice(c * FF_SUBTILE, (c + 1) * FF_SUBTILE)
        gate = conv_branch(cols, wg_ref, cwg_ref, cbg_ref, ug_ref, carry_g_ref)
        up = conv_branch(cols, wu_ref, cwu_ref, cbu_ref, uu_ref, carry_u_ref)
        act_ref[:, cols] = (jax.nn.gelu(gate, approximate=True) * up).astype(BF16)
    o_ref[...] += _dot(act_ref[...], wd_ref[...])

    @pl.when(j == pl.num_programs(1) - 1)
    def _():
        o_ref[...] = h_ref[...] + _rms(o_ref[...], g_post_ref[...])


def _ffn(l, h, g_pre, w_up, conv_w, conv_b, w_down, g_post):
    tm, tf = ROW_TILE, FF_TILE
    nj = D_FF // tf
    row = pl.BlockSpec((tm, D_MODEL), lambda i, j: (i, 0))
    gain = pl.BlockSpec((None, 1, D_MODEL), lambda i, j: (l, 0, 0))
    col = lambda r, off: pl.BlockSpec((None, r, tf), lambda i, j: (l, 0, j + off))
    return pl.pallas_call(
        _ffn_kernel,
        grid=(TOKENS // tm, nj),
        in_specs=[row, gain, col(D_MODEL, 0), col(D_MODEL, nj), col(CONV_WIDTH, 0), col(CONV_WIDTH, nj),
                  col(1, 0), col(1, nj), pl.BlockSpec((None, tf, D_MODEL), lambda i, j: (l, j, 0)), gain],
        out_specs=row,
        out_shape=jax.ShapeDtypeStruct((TOKENS, D_MODEL), F32),
        scratch_shapes=[pltpu.VMEM((tm, D_MODEL), BF16),
                        pltpu.VMEM((tm + SUBLANES, tf), F32), pltpu.VMEM((tm + SUBLANES, tf), F32),
                        pltpu.VMEM((tm, tf), BF16),
                        pltpu.VMEM((nj, SUBLANES, tf), F32), pltpu.VMEM((nj, SUBLANES, tf), F32)],
        compiler_params=_params(("arbitrary", "arbitrary")),
        name="ffn",
    )(h, g_pre, w_up, w_up, conv_w, conv_w, conv_b, conv_b, w_down, g_post)


def _ple_kernel(h_ref, p_ref, w_ple_ref, w_gate_ref, g_ref, o_ref):
    h = h_ref[...]
    gate = jax.nn.sigmoid(_dot(h.astype(BF16), w_gate_ref[...]))
    ple = _dot(p_ref[...].astype(BF16), w_ple_ref[...]) * gate
    o_ref[...] = h + _rms(ple, g_ref[...])


def _ple(l, h, p, w_ple, w_gate, g_ple):
    tm = ROW_TILE
    return pl.pallas_call(
        _ple_kernel,
        grid=(TOKENS // tm,),
        in_specs=[pl.BlockSpec((tm, D_MODEL), lambda i: (i, 0)),
                  pl.BlockSpec((None, tm, PLE_DIM), lambda i: (l, i, 0)),
                  pl.BlockSpec((None, PLE_DIM, D_MODEL), lambda i: (l, 0, 0)),
                  pl.BlockSpec((None, D_MODEL, D_MODEL), lambda i: (l, 0, 0)),
                  pl.BlockSpec((None, 1, D_MODEL), lambda i: (l, 0, 0))],
        out_specs=pl.BlockSpec((tm, D_MODEL), lambda i: (i, 0)),
        out_shape=jax.ShapeDtypeStruct((TOKENS, D_MODEL), F32),
        compiler_params=_params(("parallel",)),
        name="ple",
    )(h, p, w_ple, w_gate, g_ple)


def _rope_tables(positions, rot_dim, period):
    half = rot_dim // 2
    inv = ROPE_THETA ** (-jnp.arange(0, rot_dim, 2, dtype=F32) / rot_dim)
    ang = positions.astype(F32)[..., None] * inv
    cos = jnp.cos(ang).reshape(TOKENS, half)
    sin = jnp.sin(ang).reshape(TOKENS, half)
    rest = period - rot_dim
    ones, zeros, zh = jnp.ones((TOKENS, rest), F32), jnp.zeros((TOKENS, rest), F32), jnp.zeros_like(sin)
    c = jnp.concatenate([cos, cos, ones], axis=1)
    sa = jnp.concatenate([zh, sin, zeros], axis=1)
    sb = jnp.concatenate([-sin, zh, zeros], axis=1)
    reps = LANES // period
    return tuple(jnp.tile(t, (1, reps)) for t in (c, sa, sb))


def kernel(x, p, positions, g_mix_pre, w_in, g_q_lora, w_uq, g_kv_lora, w_ukv, lambda_q1, lambda_k1,
           lambda_q2, lambda_k2, g_diff_sub, w_o, g_mix_post, g_ffn_pre, w_up, conv_w, conv_b, w_down,
           g_ffn_post, w_ple, w_ple_gate, g_ple):
    tabs_mla = _rope_tables(positions, QK_ROPE, LANES)
    tabs_dif = _rope_tables(positions, DIFF_ROT, DIFF_D)

    vec = lambda a: a.reshape(DEPTH, 1, a.shape[-1])
    w_in_p = jnp.swapaxes(w_in, 1, 2).astype(BF16)
    uq = w_uq.reshape(DEPTH, Q_LORA, MLA_HEADS, MLA_QK_DIM)
    uq = jnp.pad(uq, ((0, 0), (0, 0), (0, 0), (0, MLA_HEAD_PAD - MLA_QK_DIM)))
    w_uq_p = uq.reshape(DEPTH, Q_LORA, MLA_HEADS * MLA_HEAD_PAD).astype(BF16)
    w_ukv_b = w_ukv.astype(BF16)
    w_o_b = w_o.astype(BF16)
    w_up_b = w_up.astype(BF16)
    w_down_b = w_down.astype(BF16)
    w_ple_b = w_ple.astype(BF16)
    w_gate_b = w_ple_gate.astype(BF16)
    p2 = p.reshape(DEPTH, TOKENS, PLE_DIM)
    g_mix_pre, g_q_lora, g_kv_lora, g_diff_sub, g_mix_post, g_ffn_pre, g_ffn_post, g_ple, conv_b = map(
        vec, (g_mix_pre, g_q_lora, g_kv_lora, g_diff_sub, g_mix_post, g_ffn_pre, g_ffn_post, g_ple, conv_b))
    lambda_q1, lambda_k1, lambda_q2, lambda_k2 = map(vec, (lambda_q1, lambda_k1, lambda_q2, lambda_k2))

    h = x.reshape(TOKENS, D_MODEL)
    for l in range(DEPTH):
        lambda_init = 0.8 - 0.6 * math.exp(-0.3 * l)
        q, kv, kr, zd = _in_proj(l, h, g_mix_pre, w_in_p, g_q_lora, w_uq_p, g_kv_lora, w_ukv_b, tabs_mla, tabs_dif)
        mla_out = _mla_attn(q, kv, kr)
        diff_out = _diff_attn(l, lambda_init, zd, lambda_q1, lambda_k1, lambda_q2, lambda_k2, g_diff_sub)
        h = _out_proj(l, h, mla_out, diff_out, w_o_b, g_mix_post)
        h = _ffn(l, h, g_ffn_pre, w_up_b, conv_w, conv_b, w_down_b, g_ffn_post)
        h = _ple(l, h, p2, w_ple_b, w_gate_b, g_ple)
    return h.reshape(BATCH, SEQ, D_MODEL)
```

```python
import functools
import math

import jax
import jax.numpy as jnp
from jax import lax
from jax.experimental import pallas as pl
from jax.experimental.pallas import tpu as pltpu

D_MODEL = 2048
BATCH = 4
SEQ = 2048
DEPTH = 4
TOKENS = BATCH * SEQ
CHUNK = 64
ROPE_THETA = 500000.0
MLA_HEADS = 8
Q_LORA = 512
KV_LORA = 256
QK_NOPE = 128
QK_ROPE = 64
V_HEAD = 128
MLA_QK_DIM = QK_NOPE + QK_ROPE
DIFF_HEADS = 8
DIFF_D = 64
DIFF_ROT = DIFF_D // 4
DIFF_QK_COLS = DIFF_HEADS * 2 * DIFF_D
DIFF_V_COLS = DIFF_HEADS * 2 * DIFF_D
MLA_IN_COLS = Q_LORA + KV_LORA + QK_ROPE
DIFF_IN_COLS = 2 * DIFF_QK_COLS + DIFF_V_COLS
D_FF = 5632
CONV_WIDTH = 3
PLE_DIM = 256
RMS_EPS = 1e-6

LANES = 128
SUBLANES = 8
MXU_COLS = 256
MLA_HEAD_PAD = 256
VMEM_LIMIT = 52 * 1024 * 1024

ROW_TILE = 512
ATTN_Q_TILE = 256
ATTN_HEADS = 4
DIFF_ATTN_HEADS = 2
FF_TILE = 512
FF_SUBTILE = 256

F32 = jnp.float32
BF16 = jnp.bfloat16
NEG = -1e30
LOG2E = math.log2(math.e)


def _rms(x, g):
    return x * lax.rsqrt(jnp.mean(x * x, axis=-1, keepdims=True) + RMS_EPS) * g


def _rope(x, c, sa, sb, half):
    return x * c + pltpu.roll(x, half, 1) * sa + pltpu.roll(x, LANES - half, 1) * sb


def _dot(a, b):
    return jnp.dot(a, b, preferred_element_type=F32)


def _dot_nt(a, b):
    return lax.dot_general(a, b, (((1,), (1,)), ((), ())), preferred_element_type=F32)


def _params(sem):
    return pltpu.CompilerParams(dimension_semantics=sem, vmem_limit_bytes=VMEM_LIMIT)


def _in_proj_kernel(h_ref, g_pre_ref, w_in_ref, g_q_ref, w_uq_ref, g_kv_ref, w_ukv_ref,
                    cm_ref, sam_ref, sbm_ref, cd_ref, sad_ref, sbd_ref,
                    q_ref, kv_ref, kr_ref, zd_ref, xn_ref):
    xn_ref[...] = _rms(h_ref[...], g_pre_ref[...]).astype(BF16)

    z = _dot_nt(xn_ref[...], w_in_ref[:MLA_IN_COLS, :])
    tabs = cm_ref[...], sam_ref[...], sbm_ref[...]
    half = QK_ROPE // 2
    scale = MLA_QK_DIM ** -0.5 * LOG2E
    cq = _rms(z[:, :Q_LORA], g_q_ref[...]).astype(BF16)
    for hd in range(MLA_HEADS):
        base = hd * MLA_HEAD_PAD
        qh = _dot(cq, w_uq_ref[:, base:base + MLA_HEAD_PAD])
        q_ref[:, base:base + QK_NOPE] = (qh[:, :QK_NOPE] * scale).astype(BF16)
        q_ref[:, base + QK_NOPE:base + MLA_HEAD_PAD] = (_rope(qh[:, QK_NOPE:], *tabs, half) * scale).astype(BF16)
    ckv = _rms(z[:, Q_LORA:Q_LORA + KV_LORA], g_kv_ref[...]).astype(BF16)
    kv_ref[...] = _dot(ckv, w_ukv_ref[...]).astype(BF16)
    kpe = z[:, Q_LORA + KV_LORA:MLA_IN_COLS]
    kpe = jnp.concatenate([kpe, jnp.zeros_like(kpe)], axis=1)
    kr_ref[...] = _rope(kpe, *tabs, half).astype(BF16)

    k_tabs = cd_ref[...], sad_ref[...], sbd_ref[...]
    q_tabs = tuple(t * (DIFF_D ** -0.5 * LOG2E) for t in k_tabs)
    half = DIFF_ROT // 2
    for t in range(DIFF_IN_COLS // MXU_COLS):
        lo = t * MXU_COLS
        zt = _dot_nt(xn_ref[...], w_in_ref[MLA_IN_COLS + lo:MLA_IN_COLS + lo + MXU_COLS, :])
        if lo >= 2 * DIFF_QK_COLS:
            zd_ref[:, lo:lo + MXU_COLS] = zt.astype(BF16)
        else:
            tabs = q_tabs if lo < DIFF_QK_COLS else k_tabs
            for u in range(MXU_COLS // LANES):
                zd_ref[:, lo + u * LANES:lo + (u + 1) * LANES] = _rope(
                    zt[:, u * LANES:(u + 1) * LANES], *tabs, half).astype(BF16)


def _in_proj(l, h, g_pre, w_in_p, g_q, w_uq, g_kv, w_ukv, tabs_mla, tabs_dif):
    tm = ROW_TILE
    row = lambda n: pl.BlockSpec((tm, n), lambda i: (i, 0))
    lay = lambda *s: pl.BlockSpec((None,) + s, lambda i: (l,) + (0,) * len(s), pipeline_mode=pl.Buffered(1))
    return pl.pallas_call(
        _in_proj_kernel,
        grid=(TOKENS // tm,),
        in_specs=[row(D_MODEL), lay(1, D_MODEL), lay(MLA_IN_COLS + DIFF_IN_COLS, D_MODEL), lay(1, Q_LORA),
                  lay(Q_LORA, MLA_HEADS * MLA_HEAD_PAD), lay(1, KV_LORA),
                  lay(KV_LORA, MLA_HEADS * (QK_NOPE + V_HEAD))] + [row(LANES)] * 6,
        out_specs=[row(MLA_HEADS * MLA_HEAD_PAD), row(MLA_HEADS * (QK_NOPE + V_HEAD)), row(LANES),
                   row(DIFF_IN_COLS)],
        out_shape=[jax.ShapeDtypeStruct((TOKENS, MLA_HEADS * MLA_HEAD_PAD), BF16),
                   jax.ShapeDtypeStruct((TOKENS, MLA_HEADS * (QK_NOPE + V_HEAD)), BF16),
                   jax.ShapeDtypeStruct((TOKENS, LANES), BF16),
                   jax.ShapeDtypeStruct((TOKENS, DIFF_IN_COLS), BF16)],
        scratch_shapes=[pltpu.VMEM((tm, D_MODEL), BF16)],
        compiler_params=_params(("parallel",)),
        name="in_proj",
    )(h, g_pre, w_in_p, g_q, w_uq, g_kv, w_ukv, *tabs_mla, *tabs_dif)


def _chunk_mask(rows):
    r = lax.broadcasted_iota(jnp.int32, (rows, ATTN_Q_TILE), 0) % ATTN_Q_TILE
    c = lax.broadcasted_iota(jnp.int32, (rows, ATTN_Q_TILE), 1)
    return (c // CHUNK) <= (r // CHUNK)


def _prefix_attention(q, load_k, load_v, qi):
    d0 = qi * ATTN_Q_TILE
    d1 = d0 + ATTN_Q_TILE
    s_d = _dot_nt(q, load_k(d0, d1))
    s_d = jnp.where(_chunk_mask(q.shape[0]), s_d, NEG)
    m = jnp.max(s_d, axis=-1, keepdims=True)
    if qi > 0:
        s_p = _dot_nt(q, load_k(0, d0))
        m = jnp.maximum(m, jnp.max(s_p, axis=-1, keepdims=True))
    pv = _dot(jnp.exp2(s_d - m).astype(BF16), load_v(d0, d1))
    if qi > 0:
        pv = pv + _dot(jnp.exp2(s_p - m).astype(BF16), load_v(0, d0))
    v_dim = pv.shape[1] // 2
    return pv[:, :v_dim] / pv[:, v_dim:]


def _mla_attn_kernel(q_ref, kv_ref, kr_ref, o_ref, kcat_ref, vcat_ref):
    kv_w = QK_NOPE + V_HEAD
    for g in range(ATTN_HEADS):
        kcat_ref[g, :, :QK_NOPE] = kv_ref[:, g * kv_w:g * kv_w + QK_NOPE]
        kcat_ref[g, :, QK_NOPE:] = kr_ref[...]
        vcat_ref[g, :, :V_HEAD] = kv_ref[:, g * kv_w + QK_NOPE:(g + 1) * kv_w]
        vcat_ref[g, :, V_HEAD:] = jnp.ones((SEQ, V_HEAD), BF16)
    for qi in range(SEQ // ATTN_Q_TILE):
        rows = slice(qi * ATTN_Q_TILE, (qi + 1) * ATTN_Q_TILE)
        for g in range(ATTN_HEADS):
            o = _prefix_attention(q_ref[rows, g * MLA_HEAD_PAD:(g + 1) * MLA_HEAD_PAD],
                                  lambda a, b, g=g: kcat_ref[g, a:b, :], lambda a, b, g=g: vcat_ref[g, a:b, :], qi)
            o_ref[rows, g * V_HEAD:(g + 1) * V_HEAD] = o.astype(BF16)


def _mla_attn(q, kv, kr):
    g = ATTN_HEADS
    return pl.pallas_call(
        _mla_attn_kernel,
        grid=(BATCH, MLA_HEADS // g),
        in_specs=[pl.BlockSpec((SEQ, g * MLA_HEAD_PAD), lambda b, h: (b, h)),
                  pl.BlockSpec((SEQ, g * (QK_NOPE + V_HEAD)), lambda b, h: (b, h)),
                  pl.BlockSpec((SEQ, LANES), lambda b, h: (b, 0))],
        out_specs=pl.BlockSpec((SEQ, g * V_HEAD), lambda b, h: (b, h)),
        out_shape=jax.ShapeDtypeStruct((TOKENS, MLA_HEADS * V_HEAD), BF16),
        scratch_shapes=[pltpu.VMEM((g, SEQ, MLA_HEAD_PAD), BF16), pltpu.VMEM((g, SEQ, 2 * V_HEAD), BF16)],
        compiler_params=_params(("parallel", "parallel")),
        name="mla_attn",
    )(q, kv, kr)


def _diff_attn_kernel(lambda_init, q_ref, k_ref, v_ref, lq1_ref, lk1_ref, lq2_ref, lk2_ref, g_ref, o_ref,
                      vcat_ref):
    tq = ATTN_Q_TILE
    hw = 2 * DIFF_D
    for g in range(DIFF_ATTN_HEADS):
        vcat_ref[g, :, :hw] = v_ref[:, g * hw:(g + 1) * hw]
        vcat_ref[g, :, hw:] = jnp.ones((SEQ, hw), BF16)
    lam = (jnp.exp(jnp.sum(lq1_ref[...] * lk1_ref[...], axis=-1, keepdims=True))
           - jnp.exp(jnp.sum(lq2_ref[...] * lk2_ref[...], axis=-1, keepdims=True)) + lambda_init)
    lane = lax.broadcasted_iota(jnp.int32, (tq, LANES), 1)
    zero = jnp.zeros((tq, LANES), BF16)
    for qi in range(SEQ // tq):
        rows = slice(qi * tq, (qi + 1) * tq)
        for g in range(DIFF_ATTN_HEADS):
            cols = slice(g * hw, (g + 1) * hw)
            q = q_ref[rows, cols]
            qq = jnp.concatenate([jnp.where(lane < DIFF_D, q, zero), jnp.where(lane >= DIFF_D, q, zero)], axis=0)
            a = _prefix_attention(qq, lambda a, b, cols=cols: k_ref[a:b, cols],
                                  lambda a, b, g=g: vcat_ref[g, a:b, :], qi)
            o = a[:tq] - lam * a[tq:]
            o_ref[rows, cols] = (_rms(o, g_ref[...]) * (1.0 - lambda_init)).astype(BF16)


def _diff_attn(l, lambda_init, zd, lq1, lk1, lq2, lk2, g_sub):
    g = DIFF_ATTN_HEADS
    hw = 2 * DIFF_D
    steps = DIFF_HEADS // g
    lay = lambda n: pl.BlockSpec((None, 1, n), lambda b, h: (l, 0, 0))
    return pl.pallas_call(
        functools.partial(_diff_attn_kernel, lambda_init),
        grid=(BATCH, steps),
        in_specs=[pl.BlockSpec((SEQ, g * hw), lambda b, h: (b, h)),
                  pl.BlockSpec((SEQ, g * hw), lambda b, h: (b, steps + h)),
                  pl.BlockSpec((SEQ, g * hw), lambda b, h: (b, 2 * steps + h)),
                  lay(DIFF_D), lay(DIFF_D), lay(DIFF_D), lay(DIFF_D), lay(hw)],
        out_specs=pl.BlockSpec((SEQ, g * hw), lambda b, h: (b, h)),
        out_shape=jax.ShapeDtypeStruct((TOKENS, DIFF_V_COLS), BF16),
        scratch_shapes=[pltpu.VMEM((g, SEQ, 2 * hw), BF16)],
        compiler_params=_params(("parallel", "parallel")),
        name="diff_attn",
    )(zd, zd, zd, lq1, lk1, lq2, lk2, g_sub)


def _out_proj_kernel(h_ref, a_ref, d_ref, wa_ref, wd_ref, g_ref, o_ref):
    mix = _dot(a_ref[...], wa_ref[...]) + _dot(d_ref[...], wd_ref[...])
    o_ref[...] = h_ref[...] + _rms(mix, g_ref[...])


def _out_proj(l, h, mla_out, diff_out, w_o, g_post):
    tm = ROW_TILE
    half = MLA_HEADS * V_HEAD
    row = lambda n: pl.BlockSpec((tm, n), lambda i: (i, 0))
    return pl.pallas_call(
        _out_proj_kernel,
        grid=(TOKENS // tm,),
        in_specs=[row(D_MODEL), row(half), row(DIFF_V_COLS),
                  pl.BlockSpec((None, half, D_MODEL), lambda i: (l, 0, 0)),
                  pl.BlockSpec((None, DIFF_V_COLS, D_MODEL), lambda i: (l, 1, 0)),
                  pl.BlockSpec((None, 1, D_MODEL), lambda i: (l, 0, 0))],
        out_specs=row(D_MODEL),
        out_shape=jax.ShapeDtypeStruct((TOKENS, D_MODEL), F32),
        compiler_params=_params(("parallel",)),
        name="out_proj",
    )(h, mla_out, diff_out, w_o, w_o, g_post)


def _ffn_kernel(h_ref, g_pre_ref, wg_ref, wu_ref, cwg_ref, cwu_ref, cbg_ref, cbu_ref, wd_ref, g_post_ref,
                o_ref, xn_ref, ug_ref, uu_ref, act_ref, carry_g_ref, carry_u_ref):
    i, j = pl.program_id(0), pl.program_id(1)
    tm = ROW_TILE
    pad = SUBLANES
    seq_start = i % (SEQ // tm) == 0

    @pl.when(j == 0)
    def _():
        xn_ref[...] = _rms(h_ref[...], g_pre_ref[...]).astype(BF16)
        o_ref[...] = jnp.zeros((tm, D_MODEL), F32)

    @pl.when(jnp.logical_and(i == 0, j == 0))
    def _():
        carry_g_ref[...] = jnp.zeros(carry_g_ref.shape, F32)
        carry_u_ref[...] = jnp.zeros(carry_u_ref.shape, F32)

    def conv_branch(cols, w_ref, cw_ref, cb_ref, u_ref, carry_ref):
        u_ref[pad:, cols] = _dot(xn_ref[...], w_ref[:, cols])
        u_ref[:pad, cols] = jnp.where(seq_start, 0.0, carry_ref[j, :, cols])
        carry_ref[j, :, cols] = u_ref[tm:, cols]
        return (cw_ref[2:3, cols] * u_ref[pad:, cols] + cw_ref[1:2, cols] * u_ref[pad - 1:tm + pad - 1, cols]
                + cw_ref[0:1, cols] * u_ref[pad - 2:tm + pad - 2, cols] + cb_ref[:, cols])

    for c in range(FF_TILE // FF_SUBTILE):
        cols = slice(c * FF_SUBTILE, (c + 1) * FF_SUBTILE)
        gate = conv_branch(cols, wg_ref, cwg_ref, cbg_ref, ug_ref, carry_g_ref)
        up = conv_branch(cols, wu_ref, cwu_ref, cbu_ref, uu_ref, carry_u_ref)
        act_ref[:, cols] = (jax.nn.gelu(gate, approximate=True) * up).astype(BF16)
    o_ref[...] += _dot(act_ref[...], wd_ref[...])

    @pl.when(j == pl.num_programs(1) - 1)
    def _():
        o_ref[...] = h_ref[...] + _rms(o_ref[...], g_post_ref[...])


def _ffn(l, h, g_pre, w_up, conv_w, conv_b, w_down, g_post):
    tm, tf = ROW_TILE, FF_TILE
    nj = D_FF // tf
    row = pl.BlockSpec((tm, D_MODEL), lambda i, j: (i, 0))
    gain = pl.BlockSpec((None, 1, D_MODEL), lambda i, j: (l, 0, 0))
    col = lambda r, off: pl.BlockSpec((None, r, tf), lambda i, j: (l, 0, j + off))
    return pl.pallas_call(
        _ffn_kernel,
        grid=(TOKENS // tm, nj),
        in_specs=[row, gain, col(D_MODEL, 0), col(D_MODEL, nj), col(CONV_WIDTH, 0), col(CONV_WIDTH, nj),
                  col(1, 0), col(1, nj), pl.BlockSpec((None, tf, D_MODEL), lambda i, j: (l, j, 0)), gain],
        out_specs=row,
        out_shape=jax.ShapeDtypeStruct((TOKENS, D_MODEL), F32),
        scratch_shapes=[pltpu.VMEM((tm, D_MODEL), BF16),
                        pltpu.VMEM((tm + SUBLANES, tf), F32), pltpu.VMEM((tm + SUBLANES, tf), F32),
                        pltpu.VMEM((tm, tf), BF16),
                        pltpu.VMEM((nj, SUBLANES, tf), F32), pltpu.VMEM((nj, SUBLANES, tf), F32)],
        compiler_params=_params(("arbitrary", "arbitrary")),
        name="ffn",
    )(h, g_pre, w_up, w_up, conv_w, conv_w, conv_b, conv_b, w_down, g_post)


def _ple_kernel(h_ref, p_ref, w_ple_ref, w_gate_ref, g_ref, o_ref):
    h = h_ref[...]
    gate = jax.nn.sigmoid(_dot(h.astype(BF16), w_gate_ref[...]))
    ple = _dot(p_ref[...].astype(BF16), w_ple_ref[...]) * gate
    o_ref[...] = h + _rms(ple, g_ref[...])


def _ple(l, h, p, w_ple, w_gate, g_ple):
    tm = ROW_TILE
    return pl.pallas_call(
        _ple_kernel,
        grid=(TOKENS // tm,),
        in_specs=[pl.BlockSpec((tm, D_MODEL), lambda i: (i, 0)),
                  pl.BlockSpec((None, tm, PLE_DIM), lambda i: (l, i, 0)),
                  pl.BlockSpec((None, PLE_DIM, D_MODEL), lambda i: (l, 0, 0)),
                  pl.BlockSpec((None, D_MODEL, D_MODEL), lambda i: (l, 0, 0)),
                  pl.BlockSpec((None, 1, D_MODEL), lambda i: (l, 0, 0))],
        out_specs=pl.BlockSpec((tm, D_MODEL), lambda i: (i, 0)),
        out_shape=jax.ShapeDtypeStruct((TOKENS, D_MODEL), F32),
        compiler_params=_params(("parallel",)),
        name="ple",
    )(h, p, w_ple, w_gate, g_ple)


def _rope_tables(positions, rot_dim, period):
    half = rot_dim // 2
    inv = ROPE_THETA ** (-jnp.arange(0, rot_dim, 2, dtype=F32) / rot_dim)
    ang = positions.astype(F32)[..., None] * inv
    cos = jnp.cos(ang).reshape(TOKENS, half)
    sin = jnp.sin(ang).reshape(TOKENS, half)
    rest = period - rot_dim
    ones, zeros, zh = jnp.ones((TOKENS, rest), F32), jnp.zeros((TOKENS, rest), F32), jnp.zeros_like(sin)
    c = jnp.concatenate([cos, cos, ones], axis=1)
    sa = jnp.concatenate([zh, sin, zeros], axis=1)
    sb = jnp.concatenate([-sin, zh, zeros], axis=1)
    reps = LANES // period
    return tuple(jnp.tile(t, (1, reps)) for t in (c, sa, sb))


def kernel(x, p, positions, g_mix_pre, w_in, g_q_lora, w_uq, g_kv_lora, w_ukv, lambda_q1, lambda_k1,
           lambda_q2, lambda_k2, g_diff_sub, w_o, g_mix_post, g_ffn_pre, w_up, conv_w, conv_b, w_down,
           g_ffn_post, w_ple, w_ple_gate, g_ple):
    tabs_mla = _rope_tables(positions, QK_ROPE, LANES)
    tabs_dif = _rope_tables(positions, DIFF_ROT, DIFF_D)

    vec = lambda a: a.reshape(DEPTH, 1, a.shape[-1])
    w_in_p = jnp.swapaxes(w_in, 1, 2).astype(BF16)
    uq = w_uq.reshape(DEPTH, Q_LORA, MLA_HEADS, MLA_QK_DIM)
    uq = jnp.pad(uq, ((0, 0), (0, 0), (0, 0), (0, MLA_HEAD_PAD - MLA_QK_DIM)))
    w_uq_p = uq.reshape(DEPTH, Q_LORA, MLA_HEADS * MLA_HEAD_PAD).astype(BF16)
    w_ukv_b = w_ukv.astype(BF16)
    w_o_b = w_o.astype(BF16)
    w_up_b = w_up.astype(BF16)
    w_down_b = w_down.astype(BF16)
    w_ple_b = w_ple.astype(BF16)
    w_gate_b = w_ple_gate.astype(BF16)
    p2 = p.reshape(DEPTH, TOKENS, PLE_DIM)
    g_mix_pre, g_q_lora, g_kv_lora, g_diff_sub, g_mix_post, g_ffn_pre, g_ffn_post, g_ple, conv_b = map(
        vec, (g_mix_pre, g_q_lora, g_kv_lora, g_diff_sub, g_mix_post, g_ffn_pre, g_ffn_post, g_ple, conv_b))
    lambda_q1, lambda_k1, lambda_q2, lambda_k2 = map(vec, (lambda_q1, lambda_k1, lambda_q2, lambda_k2))

    h = x.reshape(TOKENS, D_MODEL)
    for l in range(DEPTH):
        lambda_init = 0.8 - 0.6 * math.exp(-0.3 * l)
        q, kv, kr, zd = _in_proj(l, h, g_mix_pre, w_in_p, g_q_lora, w_uq_p, g_kv_lora, w_ukv_b, tabs_mla, tabs_dif)
        mla_out = _mla_attn(q, kv, kr)
        diff_out = _diff_attn(l, lambda_init, zd, lambda_q1, lambda_k1, lambda_q2, lambda_k2, g_diff_sub)
        h = _out_proj(l, h, mla_out, diff_out, w_o_b, g_mix_post)
        h = _ffn(l, h, g_ffn_pre, w_up_b, conv_w, conv_b, w_down_b, g_ffn_post)
        h = _ple(l, h, p2, w_ple_b, w_gate_b, g_ple)
    return h.reshape(BATCH, SEQ, D_MODEL)
```
